```python
import jax, jax.numpy as jnp
from jax import lax
import numpy as np

D_MODEL = 2048
BATCH = 4
SEQ = 8192
DEPTH = 1

GRID_W = 64
D_ATTN = D_MODEL // 2
HEAD_DIM = 128
N_ATTN_HEADS = D_ATTN // HEAD_DIM
WIN_R = 8
WIN_C = 16
D_REC = D_MODEL - D_ATTN
REC_BLOCKS = 8
REC_BLOCK_W = D_REC // REC_BLOCKS
CONV_W = 4
C_RG = 8.0
D_IN_PROJ = 3 * D_ATTN + 2 * D_REC
D_FF = -(-8 * D_MODEL // (3 * 256)) * 256
D_PLE = 256
EPS = 1e-6
NEG_INF = -1e9

kernel_name = "hymba_natten_rglru_sandwich_encoder"


def rms_norm(x, g):
    x32 = x.astype(jnp.float32)
    y = x32 * lax.rsqrt(jnp.mean(x32 * x32, axis=-1, keepdims=True) + EPS)
    return (y * g.astype(jnp.float32)).astype(x.dtype)


def neighborhood_attention(q, k, v, rpb):
    B, S, H, Dh = q.shape
    rows = S // GRID_W
    kr = min(WIN_R, rows)
    cols = jnp.arange(GRID_W)
    cstart = jnp.clip(cols - WIN_C // 2, 0, GRID_W - WIN_C)
    col_ok = (cols[None, :] >= cstart[:, None]) & (cols[None, :] < cstart[:, None] + WIN_C)
    dc_idx = jnp.clip(cols[None, :] - cols[:, None] + WIN_C - 1, 0, 2 * WIN_C - 2)
    bias_col = jnp.where(col_ok[None, None], rpb.astype(jnp.float32)[:, :, dc_idx], NEG_INF)
    scale = HEAD_DIM ** -0.5
    qg = (q * scale).reshape(B, rows, GRID_W, H, Dh)
    kg = k.reshape(B, rows, GRID_W, H, Dh)
    vg = v.reshape(B, rows, GRID_W, H, Dh)

    def row_block(r):
        rstart = jnp.clip(r - kr // 2, 0, rows - kr)
        q_r = lax.dynamic_index_in_dim(qg, r, axis=1, keepdims=False)
        k_r = lax.dynamic_slice_in_dim(kg, rstart, kr, axis=1)
        v_r = lax.dynamic_slice_in_dim(vg, rstart, kr, axis=1)
        dr_idx = rstart + jnp.arange(kr) - r + WIN_R - 1
        bias = jnp.take(bias_col, dr_idx, axis=1)
        s = jnp.einsum('bqhd,bkwhd->bhkqw', q_r, k_r,
                       preferred_element_type=jnp.float32) + bias
        pr = jax.nn.softmax(s, axis=(2, 4)).astype(v.dtype)
        return jnp.einsum('bhkqw,bkwhd->bqhd', pr, v_r)

    out = lax.map(row_block, jnp.arange(rows))
    return out.transpose(1, 0, 2, 3, 4).reshape(B, S, H * Dh)


def linear_scan(a, b):
    def combine(l, r):
        return (l[0] * r[0], r[0] * l[1] + r[1])
    return lax.associative_scan(combine, (a, b), axis=1)[1]


def rglru_bidirectional(xr, conv_w, conv_b, w_a, b_a, w_i, b_i, lam):
    B, S, _ = xr.shape
    left = CONV_W // 2
    xp = jnp.pad(xr, ((0, 0), (left, CONV_W - 1 - left), (0, 0)))
    xc = conv_b + xp[:, 0:S] * conv_w[0]
    for j in range(1, CONV_W):
        xc = xc + xp[:, j:j + S] * conv_w[j]
    xb = xc.reshape(B, S, REC_BLOCKS, REC_BLOCK_W)
    r_gate = jax.nn.sigmoid(
        jnp.einsum('bsnc,zncd->zbsnd', xb, w_a).reshape(2, B, S, D_REC) + b_a[:, None, None, :])
    i_gate = jax.nn.sigmoid(
        jnp.einsum('bsnc,zncd->zbsnd', xb, w_i).reshape(2, B, S, D_REC) + b_i[:, None, None, :])
    log_a = -C_RG * r_gate.astype(jnp.float32) * jax.nn.softplus(-lam.astype(jnp.float32))[:, None, None, :]
    a = jnp.exp(log_a)
    bterm = jnp.sqrt(-jnp.expm1(2.0 * log_a)) * i_gate.astype(jnp.float32) * xc.astype(jnp.float32)[None]
    h_fwd = linear_scan(a[0], bterm[0])
    h_bwd = jnp.flip(linear_scan(jnp.flip(a[1], axis=1), jnp.flip(bterm[1], axis=1)), axis=1)
    return (h_fwd + h_bwd).astype(xr.dtype)


def setup_inputs(seed: int = 0) -> dict:
    key = jax.random.key(seed)
    ks = jax.random.split(key, 32)
    f32 = jnp.float32

    def nrm(k, shape, scale):
        return jax.random.normal(k, shape, f32) * scale

    def gain(k, n):
        return 1.0 + 0.02 * jax.random.normal(k, (DEPTH, n), f32)

    u = jax.random.uniform(ks[10], (DEPTH, 2, D_REC), f32, minval=0.9, maxval=0.999)
    a0 = u ** (1.0 / C_RG)
    lam = jnp.log(a0) - jnp.log1p(-a0)
    return {
        "x": nrm(ks[0], (BATCH, SEQ, D_MODEL), 1.0),
        "p": nrm(ks[1], (DEPTH, BATCH, SEQ, D_PLE), 1.0),
        "g_mix_pre": gain(ks[2], D_MODEL),
        "w_in": nrm(ks[3], (DEPTH, D_MODEL, D_IN_PROJ), D_MODEL ** -0.5),
        "rpb": nrm(ks[4], (DEPTH, N_ATTN_HEADS, 2 * WIN_R - 1, 2 * WIN_C - 1), 0.02),
        "conv_w": nrm(ks[5], (DEPTH, CONV_W, D_REC), CONV_W ** -0.5),
        "conv_b": nrm(ks[6], (DEPTH, D_REC), 0.01),
        "w_rg_a": nrm(ks[7], (DEPTH, 2, REC_BLOCKS, REC_BLOCK_W, REC_BLOCK_W), REC_BLOCK_W ** -0.5),
        "b_rg_a": nrm(ks[8], (DEPTH, 2, D_REC), 0.01),
        "w_rg_i": nrm(ks[9], (DEPTH, 2, REC_BLOCKS, REC_BLOCK_W, REC_BLOCK_W), REC_BLOCK_W ** -0.5),
        "b_rg_i": nrm(ks[11], (DEPTH, 2, D_REC), 0.01),
        "lam": lam,
        "g_attn_out": gain(ks[12], D_ATTN),
        "g_rec_out": gain(ks[13], D_REC),
        "w_out": nrm(ks[14], (DEPTH, D_ATTN + D_REC, D_MODEL), (D_ATTN + D_REC) ** -0.5),
        "g_mix_post": gain(ks[15], D_MODEL),
        "g_ffn_pre": gain(ks[16], D_MODEL),
        "w_ffn_gate": nrm(ks[17], (DEPTH, D_MODEL, D_FF), D_MODEL ** -0.5),
        "w_ffn_up": nrm(ks[18], (DEPTH, D_MODEL, D_FF), D_MODEL ** -0.5),
        "w_ffn_down": nrm(ks[19], (DEPTH, D_FF, D_MODEL), D_FF ** -0.5),
        "g_ffn_post": gain(ks[20], D_MODEL),
        "g_ple_pre": gain(ks[21], D_MODEL),
        "w_ple_gate": nrm(ks[22], (DEPTH, D_MODEL, D_MODEL), D_MODEL ** -0.5),
        "w_ple_proj": nrm(ks[23], (DEPTH, D_PLE, D_MODEL), D_PLE ** -0.5),
        "g_ple_post": gain(ks[24], D_MODEL),
    }


def reference(x, p, g_mix_pre, w_in, rpb, conv_w, conv_b, w_rg_a, b_rg_a, w_rg_i, b_rg_i,
              lam, g_attn_out, g_rec_out, w_out, g_mix_post, g_ffn_pre, w_ffn_gate,
              w_ffn_up, w_ffn_down, g_ffn_post, g_ple_pre, w_ple_gate, w_ple_proj, g_ple_post):
    B, S, _ = x.shape
    h = x
    for i in range(DEPTH):
        hn = rms_norm(h, g_mix_pre[i])
        u = hn @ w_in[i]
        q, k, v, xr, yg = jnp.split(
            u, [D_ATTN, 2 * D_ATTN, 3 * D_ATTN, 3 * D_ATTN + D_REC], axis=-1)
        attn = neighborhood_attention(
            q.reshape(B, S, N_ATTN_HEADS, HEAD_DIM),
            k.reshape(B, S, N_ATTN_HEADS, HEAD_DIM),
            v.reshape(B, S, N_ATTN_HEADS, HEAD_DIM), rpb[i])
        rec = rglru_bidirectional(xr, conv_w[i], conv_b[i], w_rg_a[i], b_rg_a[i],
                                  w_rg_i[i], b_rg_i[i], lam[i]) * jax.nn.gelu(yg)
        mixed = jnp.concatenate(
            [rms_norm(attn, g_attn_out[i]), rms_norm(rec, g_rec_out[i])], axis=-1) @ w_out[i]
        h = h + rms_norm(mixed, g_mix_post[i])
        fn = rms_norm(h, g_ffn_pre[i])
        ff = (jax.nn.silu(fn @ w_ffn_gate[i]) * (fn @ w_ffn_up[i])) @ w_ffn_down[i]
        h = h + rms_norm(ff, g_ffn_post[i])
        gate = jax.nn.sigmoid(rms_norm(h, g_ple_pre[i]) @ w_ple_gate[i])
        ple = p[i] @ w_ple_proj[i]
        h = h + rms_norm(gate * ple, g_ple_post[i])
    return h
```

```python
import functools

import jax
import jax.numpy as jnp
from jax import lax
from jax.experimental import pallas as pl
from jax.experimental.pallas import tpu as pltpu

F32 = jnp.float32
BF16 = jnp.bfloat16

EPS = 1e-6
NEG_INF = -1e9
GRID_W = 64
HEAD_DIM = 128
WIN_R = 8
WIN_C = 16
CONV_W = 4
C_RG = 8.0
REC_BLOCK_W = 128

V7X_SUBLANES = 8
V7X_VMEM_BYTES = 64 * 1024 * 1024
V7X_VMEM_CAP = V7X_VMEM_BYTES - 8 * 1024 * 1024

Q_ROWS = 4
Q_TOK = Q_ROWS * GRID_W
K_BLOCKS = 3

SCAN_CHUNK = 256
SCAN_PITCH = SCAN_CHUNK + V7X_SUBLANES


def _vmem_limit(estimate_bytes):
    return int(min(estimate_bytes * 5 // 4 + (4 << 20), V7X_VMEM_CAP))


def _params(semantics, estimate_bytes):
    return pltpu.CompilerParams(dimension_semantics=semantics,
                                vmem_limit_bytes=_vmem_limit(estimate_bytes))


def _rms(x, g):
    ms = jnp.mean(x * x, axis=-1, keepdims=True)
    return x * lax.rsqrt(ms + EPS) * g


def _inproj_kernel(x_ref, g_ref, w_ref, o_ref, hn_ref, *, row_chunk):
    @pl.when(pl.program_id(1) == 0)
    def _():
        def body(c, carry):
            r = pl.multiple_of(c * row_chunk, row_chunk)
            hn_ref[pl.ds(r, row_chunk), :] = _rms(x_ref[pl.ds(r, row_chunk), :], g_ref[...]).astype(BF16)
            return carry
        lax.fori_loop(0, x_ref.shape[0] // row_chunk, body, 0)

    o_ref[...] = jnp.dot(hn_ref[...], w_ref[...], preferred_element_type=F32).astype(o_ref.dtype)


def _in_proj(x2, g, w, *, tm, tn):
    t, d = x2.shape
    n = w.shape[1]
    est = 2 * tm * d * 4 + 2 * d * tn * 2 + 2 * tm * tn * 2 + tm * d * 2
    return pl.pallas_call(
        functools.partial(_inproj_kernel, row_chunk=min(128, tm)),
        out_shape=jax.ShapeDtypeStruct((t, n), BF16),
        grid=(t // tm, n // tn),
        in_specs=[pl.BlockSpec((tm, d), lambda i, j: (i, 0)),
                  pl.BlockSpec((1, d), lambda i, j: (0, 0)),
                  pl.BlockSpec((d, tn), lambda i, j: (0, j))],
        out_specs=pl.BlockSpec((tm, tn), lambda i, j: (i, j)),
        scratch_shapes=[pltpu.VMEM((tm, d), BF16)],
        compiler_params=_params(("parallel", "arbitrary"), est),
        name="in_proj",
    )(x2, g, w)


def _attn_bias(rpb, rows):
    nblk = rows // Q_ROWS
    kr = min(WIN_R, rows)
    q_row0 = jnp.array([0, Q_ROWS, rows - Q_ROWS])
    k_row0 = jnp.array([0, 0, (nblk - K_BLOCKS) * Q_ROWS])
    qr = q_row0[:, None] + jnp.arange(Q_ROWS)[None, :]
    kro = k_row0[:, None] + jnp.arange(K_BLOCKS * Q_ROWS)[None, :]
    rstart = jnp.clip(qr - kr // 2, 0, rows - kr)
    row_ok = (kro[:, None, :] >= rstart[:, :, None]) & (kro[:, None, :] < rstart[:, :, None] + kr)
    dr = jnp.clip(kro[:, None, :] - qr[:, :, None] + WIN_R - 1, 0, 2 * WIN_R - 2)
    cols = jnp.arange(GRID_W)
    cstart = jnp.clip(cols - WIN_C // 2, 0, GRID_W - WIN_C)
    col_ok = (cols[None, :] >= cstart[:, None]) & (cols[None, :] < cstart[:, None] + WIN_C)
    dc = jnp.clip(cols[None, :] - cols[:, None] + WIN_C - 1, 0, 2 * WIN_C - 2)
    b = rpb.astype(F32)[:, dr[:, :, None, :, None], dc[None, None, :, None, :]]
    ok = row_ok[:, :, None, :, None] & col_ok[None, None, :, None, :]
    b = jnp.where(ok[None], b, NEG_INF)
    h = rpb.shape[0]
    return b.transpose(1, 0, 2, 3, 4, 5).reshape(3, h, Q_TOK, K_BLOCKS * Q_TOK)


def _attn_kernel(q_ref, k0_ref, k1_ref, k2_ref, v0_ref, v1_ref, v2_ref, bias_ref, o_ref, *, n_heads):
    k_refs = (k0_ref, k1_ref, k2_ref)
    v_refs = (v0_ref, v1_ref, v2_ref)
    scale = HEAD_DIM ** -0.5
    for h in range(n_heads):
        hs = slice(h * HEAD_DIM, (h + 1) * HEAD_DIM)
        q = (q_ref[:, hs].astype(F32) * scale).astype(BF16)
        s = [lax.dot_general(q, k_refs[i][:, hs], (((1,), (1,)), ((), ())),
                             preferred_element_type=F32)
             + bias_ref[h, :, i * Q_TOK:(i + 1) * Q_TOK] for i in range(K_BLOCKS)]
        m = s[0].max(axis=-1, keepdims=True)
        for i in range(1, K_BLOCKS):
            m = jnp.maximum(m, s[i].max(axis=-1, keepdims=True))
        p = [jnp.exp(si - m) for si in s]
        l = p[0].sum(axis=-1, keepdims=True)
        for i in range(1, K_BLOCKS):
            l = l + p[i].sum(axis=-1, keepdims=True)
        inv = 1.0 / l
        acc = jnp.dot((p[0] * inv).astype(BF16), v_refs[0][:, hs], preferred_element_type=F32)
        for i in range(1, K_BLOCKS):
            acc = acc + jnp.dot((p[i] * inv).astype(BF16), v_refs[i][:, hs], preferred_element_type=F32)
        o_ref[:, hs] = acc.astype(o_ref.dtype)


def _attention(u, bias, *, batch, seq, d_attn):
    t = u.shape[0]
    n_heads = d_attn // HEAD_DIM
    nblk = seq // Q_TOK
    assert nblk >= K_BLOCKS

    def q_map(b, j):
        return (b * nblk + j, 0)

    def kv_map(i, col):
        def f(b, j):
            return (b * nblk + jnp.clip(j - 1, 0, nblk - K_BLOCKS) + i, col)
        return f

    def bias_map(b, j):
        return (jnp.where(j == 0, 0, jnp.where(j == nblk - 1, 2, 1)), 0, 0, 0)

    blk = pl.BlockSpec((Q_TOK, d_attn), q_map)
    in_specs = [blk]
    in_specs += [pl.BlockSpec((Q_TOK, d_attn), kv_map(i, 1)) for i in range(K_BLOCKS)]
    in_specs += [pl.BlockSpec((Q_TOK, d_attn), kv_map(i, 2)) for i in range(K_BLOCKS)]
    in_specs += [pl.BlockSpec((None, n_heads, Q_TOK, K_BLOCKS * Q_TOK), bias_map)]
    est = 2 * 8 * Q_TOK * d_attn * 2 + 2 * n_heads * Q_TOK * K_BLOCKS * Q_TOK * 4
    return pl.pallas_call(
        functools.partial(_attn_kernel, n_heads=n_heads),
        out_shape=jax.ShapeDtypeStruct((t, d_attn), BF16),
        grid=(batch, nblk),
        in_specs=in_specs,
        out_specs=blk,
        compiler_params=_params(("parallel", "arbitrary"), est),
        name="nattn",
    )(u, u, u, u, u, u, u, bias)


def _rglru_kernel(xr_ref, yg_ref, cw_ref, cb_ref, wg_ref, bg_ref, lam_ref, o_ref,
                  xp_ref, a_ref, b_ref, he_ref, pe_ref, cr_ref, *, seq):
    n_chunks = seq // SCAN_CHUNK
    n_groups = n_chunks // V7X_SUBLANES
    w = REC_BLOCK_W
    halo = V7X_SUBLANES

    zeros = jnp.zeros((halo, w), F32)
    xp_ref[0:halo, :] = zeros
    xp_ref[seq + halo:seq + 2 * halo, :] = zeros

    def stage(c, carry):
        r = pl.multiple_of(c * SCAN_CHUNK, SCAN_CHUNK)
        xp_ref[pl.ds(r + halo, SCAN_CHUNK), :] = xr_ref[pl.ds(r, SCAN_CHUNK), :].astype(F32)
        return carry
    lax.fori_loop(0, n_chunks, stage, 0)

    z = -lam_ref[...]
    coef = -C_RG * (jnp.maximum(z, 0.0) + jnp.log1p(jnp.exp(-jnp.abs(z))))

    left = CONV_W // 2

    def gates(c, carry):
        r = pl.multiple_of(c * SCAN_CHUNK, SCAN_CHUNK)
        win = xp_ref[pl.ds(r, SCAN_CHUNK + 2 * halo), :]
        xc = cb_ref[...] + win[halo - left:halo - left + SCAN_CHUNK] * cw_ref[0:1, :]
        for j in range(1, CONV_W):
            o = halo - left + j
            xc = xc + win[o:o + SCAN_CHUNK] * cw_ref[j:j + 1, :]
        g = jnp.dot(xc.astype(BF16), wg_ref[...], preferred_element_type=F32) + bg_ref[...]
        g = jax.nn.sigmoid(g)
        ro = pl.multiple_of(c * SCAN_PITCH, V7X_SUBLANES)
        for d in range(2):
            a = jnp.exp(g[:, d * w:(d + 1) * w] * coef[d:d + 1, :])
            bt = jnp.sqrt(jnp.maximum(1.0 - a * a, 0.0)) * g[:, (2 + d) * w:(3 + d) * w] * xc
            a_ref[d, pl.ds(ro, SCAN_CHUNK), :] = a
            b_ref[d, pl.ds(ro, SCAN_CHUNK), :] = bt
        return carry
    lax.fori_loop(0, n_chunks, gates, 0)

    def rows(d, g, t):
        pos = t if d == 0 else SCAN_CHUNK - 1 - t
        return pl.ds(g * V7X_SUBLANES * SCAN_PITCH + pos, V7X_SUBLANES, stride=SCAN_PITCH)

    def pass1(t, carry):
        hs, ps = carry
        nh, np_ = [], []
        for d in range(2):
            for g in range(n_groups):
                i = d * n_groups + g
                a = a_ref[d, rows(d, g, t), :]
                b = b_ref[d, rows(d, g, t), :]
                nh.append(a * hs[i] + b)
                np_.append(a * ps[i])
        return tuple(nh), tuple(np_)

    n_state = 2 * n_groups
    zero8 = jnp.zeros((V7X_SUBLANES, w), F32)
    hs, ps = lax.fori_loop(0, SCAN_CHUNK, pass1,
                           ((zero8,) * n_state, (zero8 + 1.0,) * n_state))
    for d in range(2):
        for g in range(n_groups):
            he_ref[d, g * V7X_SUBLANES:(g + 1) * V7X_SUBLANES, :] = hs[d * n_groups + g]
            pe_ref[d, g * V7X_SUBLANES:(g + 1) * V7X_SUBLANES, :] = ps[d * n_groups + g]

    for d in range(2):
        he = he_ref[d]
        pe = pe_ref[d]
        order = range(n_chunks) if d == 0 else range(n_chunks - 1, -1, -1)
        carry = jnp.zeros((1, w), F32)
        for j in order:
            cr_ref[d, j:j + 1, :] = carry
            carry = he[j:j + 1, :] + pe[j:j + 1, :] * carry

    def pass2(t, hs):
        nh = []
        for d in range(2):
            for g in range(n_groups):
                i = d * n_groups + g
                a = a_ref[d, rows(d, g, t), :]
                b = b_ref[d, rows(d, g, t), :]
                h = a * hs[i] + b
                b_ref[d, rows(d, g, t), :] = h
                nh.append(h)
        return tuple(nh)

    init = tuple(cr_ref[d, g * V7X_SUBLANES:(g + 1) * V7X_SUBLANES, :]
                 for d in range(2) for g in range(n_groups))
    lax.fori_loop(0, SCAN_CHUNK, pass2, init)

    def combine(c, carry):
        r = pl.multiple_of(c * SCAN_CHUNK, SCAN_CHUNK)
        ro = pl.multiple_of(c * SCAN_PITCH, V7X_SUBLANES)
        h = b_ref[0, pl.ds(ro, SCAN_CHUNK), :] + b_ref[1, pl.ds(ro, SCAN_CHUNK), :]
        y = yg_ref[pl.ds(r, SCAN_CHUNK), :].astype(F32)
        o_ref[pl.ds(r, SCAN_CHUNK), :] = (h * jax.nn.gelu(y)).astype(o_ref.dtype)
        return carry
    lax.fori_loop(0, n_chunks, combine, 0)


def _rglru(u3, conv_w, conv_b, wg, bg, lam, *, d_attn, d_rec):
    b, s, _ = u3.shape
    w = REC_BLOCK_W
    nb = d_rec // w
    assert s % (SCAN_CHUNK * V7X_SUBLANES) == 0
    n_chunks = s // SCAN_CHUNK
    x_col0 = 3 * d_attn // w
    y_col0 = x_col0 + nb
    est = (3 * 2 * s * w * 2 + (s + 2 * V7X_SUBLANES) * w * 4
           + 2 * 2 * n_chunks * SCAN_PITCH * w * 4)
    return pl.pallas_call(
        functools.partial(_rglru_kernel, seq=s),
        out_shape=jax.ShapeDtypeStruct((b, s, d_rec), BF16),
        grid=(b, nb),
        in_specs=[pl.BlockSpec((None, s, w), lambda i, n: (i, 0, x_col0 + n)),
                  pl.BlockSpec((None, s, w), lambda i, n: (i, 0, y_col0 + n)),
                  pl.BlockSpec((CONV_W, w), lambda i, n: (0, n)),
                  pl.BlockSpec((1, w), lambda i, n: (0, n)),
                  pl.BlockSpec((None, w, 4 * w), lambda i, n: (n, 0, 0)),
                  pl.BlockSpec((None, 1, 4 * w), lambda i, n: (n, 0, 0)),
                  pl.BlockSpec((2, w), lambda i, n: (0, n))],
        out_specs=pl.BlockSpec((None, s, w), lambda i, n: (i, 0, n)),
        scratch_shapes=[pltpu.VMEM((s + 2 * V7X_SUBLANES, w), F32),
                        pltpu.VMEM((2, n_chunks * SCAN_PITCH, w), F32),
                        pltpu.VMEM((2, n_chunks * SCAN_PITCH, w), F32),
                        pltpu.VMEM((2, n_chunks, w), F32),
                        pltpu.VMEM((2, n_chunks, w), F32),
                        pltpu.VMEM((2, n_chunks, w), F32)],
        compiler_params=_params(("parallel", "arbitrary"), est),
        name="rglru",
    )(u3, u3, conv_w, conv_b, wg, bg, lam)


def _outproj_kernel(a_ref, r_ref, x_ref, ga_ref, gr_ref, w_ref, gp_ref, o_ref, cat_ref):
    da = a_ref.shape[1]
    cat_ref[:, :da] = _rms(a_ref[...].astype(F32), ga_ref[...]).astype(BF16)
    cat_ref[:, da:] = _rms(r_ref[...].astype(F32), gr_ref[...]).astype(BF16)
    mixed = jnp.dot(cat_ref[...], w_ref[...], preferred_element_type=F32)
    o_ref[...] = x_ref[...] + _rms(mixed, gp_ref[...])


def _out_proj(attn, rec, x2, g_attn, g_rec, w, g_post, *, tm):
    t, d = x2.shape
    da, dr = attn.shape[1], rec.shape[1]
    est = 2 * tm * (da + dr) * 2 + 4 * tm * d * 4 + (da + dr) * d * 2 + tm * (da + dr) * 2
    row = lambda i: (i, 0)
    fixed = lambda i: (0, 0)
    return pl.pallas_call(
        _outproj_kernel,
        out_shape=jax.ShapeDtypeStruct((t, d), F32),
        grid=(t // tm,),
        in_specs=[pl.BlockSpec((tm, da), row), pl.BlockSpec((tm, dr), row), pl.BlockSpec((tm, d), row),
                  pl.BlockSpec((1, da), fixed), pl.BlockSpec((1, dr), fixed),
                  pl.BlockSpec((da + dr, d), fixed, pipeline_mode=pl.Buffered(1)),
                  pl.BlockSpec((1, d), fixed)],
        out_specs=pl.BlockSpec((tm, d), row),
        scratch_shapes=[pltpu.VMEM((tm, da + dr), BF16)],
        compiler_params=_params(("parallel",), est),
        name="out_proj",
    )(attn, rec, x2, g_attn, g_rec, w, g_post)


def _ffn_kernel(h_ref, gpre_ref, wg_ref, wu_ref, wd_ref, gpost_ref, o_ref, fn_ref, acc_ref, *, row_chunk):
    f = pl.program_id(1)

    @pl.when(f == 0)
    def _():
        def body(c, carry):
            r = pl.multiple_of(c * row_chunk, row_chunk)
            fn_ref[pl.ds(r, row_chunk), :] = _rms(h_ref[pl.ds(r, row_chunk), :], gpre_ref[...]).astype(BF16)
            return carry
        lax.fori_loop(0, h_ref.shape[0] // row_chunk, body, 0)

    fn = fn_ref[...]
    gate = jnp.dot(fn, wg_ref[...], preferred_element_type=F32)
    up = jnp.dot(fn, wu_ref[...], preferred_element_type=F32)
    act = (jax.nn.silu(gate) * up).astype(BF16)
    part = jnp.dot(act, wd_ref[...], preferred_element_type=F32)

    @pl.when(f == 0)
    def _():
        acc_ref[...] = part

    @pl.when(f > 0)
    def _():
        acc_ref[...] += part

    @pl.when(f == pl.num_programs(1) - 1)
    def _():
        def body(c, carry):
            r = pl.multiple_of(c * row_chunk, row_chunk)
            o_ref[pl.ds(r, row_chunk), :] = _rms(acc_ref[pl.ds(r, row_chunk), :], gpost_ref[...]).astype(o_ref.dtype)
            return carry
        lax.fori_loop(0, h_ref.shape[0] // row_chunk, body, 0)


def _ffn(h1, g_pre, w_gate, w_up, w_down, g_post, *, tm, tf):
    t, d = h1.shape
    dff = w_gate.shape[1]
    est = 2 * tm * d * 4 + 3 * 2 * d * tf * 2 + 2 * tm * d * 2 + tm * d * 2 + tm * d * 4
    return pl.pallas_call(
        functools.partial(_ffn_kernel, row_chunk=min(128, tm)),
        out_shape=jax.ShapeDtypeStruct((t, d), BF16),
        grid=(t // tm, dff // tf),
        in_specs=[pl.BlockSpec((tm, d), lambda i, f: (i, 0)),
                  pl.BlockSpec((1, d), lambda i, f: (0, 0)),
                  pl.BlockSpec((d, tf), lambda i, f: (0, f)),
                  pl.BlockSpec((d, tf), lambda i, f: (0, f)),
                  pl.BlockSpec((tf, d), lambda i, f: (f, 0)),
                  pl.BlockSpec((1, d), lambda i, f: (0, 0))],
        out_specs=pl.BlockSpec((tm, d), lambda i, f: (i, 0)),
        scratch_shapes=[pltpu.VMEM((tm, d), BF16), pltpu.VMEM((tm, d), F32)],
        compiler_params=_params(("parallel", "arbitrary"), est),
        name="ffn",
    )(h1, g_pre, w_gate, w_up, w_down, g_post)


def _ple_kernel(h_ref, d_ref, p_ref, gpre_ref, wg_ref, wp_ref, gpost_ref, o_ref):
    h2 = h_ref[...] + d_ref[...].astype(F32)
    gate = jax.nn.sigmoid(jnp.dot(_rms(h2, gpre_ref[...]).astype(BF16), wg_ref[...],
                                  preferred_element_type=F32))
    ple = jnp.dot(p_ref[...].astype(BF16), wp_ref[...], preferred_element_type=F32)
    o_ref[...] = h2 + _rms(gate * ple, gpost_ref[...])


def _ple(h1, dff, p2, g_pre, w_gate, w_proj, g_post, *, tm):
    t, d = h1.shape
    dp = p2.shape[1]
    est = 4 * tm * d * 4 + 2 * tm * d * 2 + 2 * tm * dp * 4 + d * d * 2 + 2 * dp * d * 2 + 3 * tm * d * 4
    row = lambda i: (i, 0)
    fixed = lambda i: (0, 0)
    return pl.pallas_call(
        _ple_kernel,
        out_shape=jax.ShapeDtypeStruct((t, d), F32),
        grid=(t // tm,),
        in_specs=[pl.BlockSpec((tm, d), row), pl.BlockSpec((tm, d), row), pl.BlockSpec((tm, dp), row),
                  pl.BlockSpec((1, d), fixed),
                  pl.BlockSpec((d, d), fixed, pipeline_mode=pl.Buffered(1)),
                  pl.BlockSpec((dp, d), fixed),
                  pl.BlockSpec((1, d), fixed)],
        out_specs=pl.BlockSpec((tm, d), row),
        compiler_params=_params(("parallel",), est),
        name="ple",
    )(h1, dff, p2, g_pre, w_gate, w_proj, g_post)


def _layer(h, p_i, g_mix_pre, w_in, rpb, conv_w, conv_b, w_rg_a, b_rg_a, w_rg_i, b_rg_i, lam,
           g_attn_out, g_rec_out, w_out, g_mix_post, g_ffn_pre, w_ffn_gate, w_ffn_up, w_ffn_down,
           g_ffn_post, g_ple_pre, w_ple_gate, w_ple_proj, g_ple_post):
    batch, seq, d = h.shape
    t = batch * seq
    d_attn = g_attn_out.shape[0]
    d_rec = g_rec_out.shape[0]
    nb = d_rec // REC_BLOCK_W
    row = lambda v: v.reshape(1, -1).astype(F32)

    x2 = h.reshape(t, d)
    u = _in_proj(x2, row(g_mix_pre), w_in.astype(BF16), tm=min(1024, t), tn=1024)

    attn = _attention(u, _attn_bias(rpb, seq // GRID_W), batch=batch, seq=seq, d_attn=d_attn)

    wg = jnp.concatenate([w_rg_a[0], w_rg_a[1], w_rg_i[0], w_rg_i[1]], axis=-1).astype(BF16)
    bg = jnp.concatenate([b_rg_a.reshape(2, nb, 1, REC_BLOCK_W)[0], b_rg_a.reshape(2, nb, 1, REC_BLOCK_W)[1],
                          b_rg_i.reshape(2, nb, 1, REC_BLOCK_W)[0], b_rg_i.reshape(2, nb, 1, REC_BLOCK_W)[1]],
                         axis=-1).astype(F32)
    rec = _rglru(u.reshape(batch, seq, -1), conv_w.astype(F32), row(conv_b), wg, bg, lam.astype(F32),
                 d_attn=d_attn, d_rec=d_rec)

    h1 = _out_proj(attn, rec.reshape(t, d_rec), x2, row(g_attn_out), row(g_rec_out),
                   w_out.astype(BF16), row(g_mix_post), tm=min(512, t))

    dff = _ffn(h1, row(g_ffn_pre), w_ffn_gate.astype(BF16), w_ffn_up.astype(BF16),
               w_ffn_down.astype(BF16), row(g_ffn_post), tm=min(1024, t), tf=512)

    out = _ple(h1, dff, p_i.reshape(t, -1), row(g_ple_pre), w_ple_gate.astype(BF16),
               w_ple_proj.astype(BF16), row(g_ple_post), tm=min(512, t))
    return out.reshape(batch, seq, d)


def kernel(x, p, g_mix_pre, w_in, rpb, conv_w, conv_b, w_rg_a, b_rg_a, w_rg_i, b_rg_i, lam, g_attn_out, g_rec_out, w_out, g_mix_post, g_ffn_pre, w_ffn_gate, w_ffn_up, w_ffn_down, g_ffn_post, g_ple_pre, w_ple_gate, w_ple_proj, g_ple_post):
    h = x
    for i in range(p.shape[0]):
        h = _layer(h, p[i], g_mix_pre[i], w_in[i], rpb[i], conv_w[i], conv_b[i], w_rg_a[i], b_rg_a[i],
                   w_rg_i[i], b_rg_i[i], lam[i], g_attn_out[i], g_rec_out[i], w_out[i], g_mix_post[i],
                   g_ffn_pre[i], w_ffn_gate[i], w_ffn_up[i], w_ffn_down[i], g_ffn_post[i], g_ple_pre[i],
                   w_ple_gate[i], w_ple_proj[i], g_ple_post[i])
    return h
```

```python
import functools

import numpy as np
import jax
import jax.numpy as jnp
from jax import lax
from jax.experimental import pallas as pl
from jax.experimental.pallas import tpu as pltpu

F32 = jnp.float32
BF16 = jnp.bfloat16

EPS = 1e-6
NEG_INF = -1e9
GRID_W = 64
HEAD_DIM = 128
WIN_R = 8
WIN_C = 16
CONV_W = 4
C_RG = 8.0
REC_BLOCK_W = 128

V7X_SUBLANES = 8
V7X_BF16_ROWS = 16
V7X_VMEM_BYTES = 64 * 1024 * 1024
V7X_VMEM_CAP = V7X_VMEM_BYTES - 8 * 1024 * 1024

Q_ROWS = 4
Q_TOK = Q_ROWS * GRID_W
K_BLOCKS = 3

SCAN_CHUNK = 256
SCAN_UNROLL = 8


def _vmem_limit(estimate_bytes):
    return int(min(estimate_bytes * 5 // 4 + (4 << 20), V7X_VMEM_CAP))


def _params(semantics, estimate_bytes):
    return pltpu.CompilerParams(dimension_semantics=semantics,
                                vmem_limit_bytes=_vmem_limit(estimate_bytes))


def _rms(x, g):
    ms = jnp.mean(x * x, axis=-1, keepdims=True)
    return x * lax.rsqrt(ms + EPS) * g


def _inproj_kernel(x_ref, g_ref, w_ref, o_ref, hn_ref, *, row_chunk):
    @pl.when(pl.program_id(1) == 0)
    def _():
        def body(c, carry):
            r = pl.multiple_of(c * row_chunk, row_chunk)
            hn_ref[pl.ds(r, row_chunk), :] = _rms(x_ref[pl.ds(r, row_chunk), :], g_ref[...]).astype(BF16)
            return carry
        lax.fori_loop(0, x_ref.shape[0] // row_chunk, body, 0)

    o_ref[...] = jnp.dot(hn_ref[...], w_ref[...], preferred_element_type=F32).astype(o_ref.dtype)


def _in_proj(x2, g, w, *, tm, tn):
    t, d = x2.shape
    n = w.shape[1]
    est = 2 * tm * d * 4 + 2 * d * tn * 2 + 2 * tm * tn * 2 + tm * d * 2
    return pl.pallas_call(
        functools.partial(_inproj_kernel, row_chunk=min(128, tm)),
        out_shape=jax.ShapeDtypeStruct((t, n), BF16),
        grid=(t // tm, n // tn),
        in_specs=[pl.BlockSpec((tm, d), lambda i, j: (i, 0)),
                  pl.BlockSpec((1, d), lambda i, j: (0, 0)),
                  pl.BlockSpec((d, tn), lambda i, j: (0, j))],
        out_specs=pl.BlockSpec((tm, tn), lambda i, j: (i, j)),
        scratch_shapes=[pltpu.VMEM((tm, d), BF16)],
        compiler_params=_params(("parallel", "arbitrary"), est),
        name="in_proj",
    )(x2, g, w)


def _attn_bias(rpb, rows):
    nblk = rows // Q_ROWS
    kr = min(WIN_R, rows)
    n_dr, n_dc = 2 * WIN_R - 1, 2 * WIN_C - 1
    q_row0 = np.array([0, Q_ROWS, rows - Q_ROWS])
    k_row0 = np.array([0, 0, (nblk - K_BLOCKS) * Q_ROWS])
    qr = q_row0[:, None] + np.arange(Q_ROWS)[None, :]
    kro = k_row0[:, None] + np.arange(K_BLOCKS * Q_ROWS)[None, :]
    rstart = np.clip(qr - kr // 2, 0, rows - kr)
    row_ok = (kro[:, None, :] >= rstart[:, :, None]) & (kro[:, None, :] < rstart[:, :, None] + kr)
    dr = np.clip(kro[:, None, :] - qr[:, :, None] + WIN_R - 1, 0, n_dr - 1)
    cols = np.arange(GRID_W)
    cstart = np.clip(cols - WIN_C // 2, 0, GRID_W - WIN_C)
    col_ok = (cols[None, :] >= cstart[:, None]) & (cols[None, :] < cstart[:, None] + WIN_C)
    dc = np.clip(cols[None, :] - cols[:, None] + WIN_C - 1, 0, n_dc - 1)
    pick_dc = (dc[None] == np.arange(n_dc)[:, None, None]).astype(np.float32)
    pick_dr = (dr[..., None] == np.arange(n_dr)).astype(np.float32)
    exact = lax.Precision.HIGHEST
    by_col = jnp.einsum("hrd,dqk->hrqk", rpb.astype(F32), pick_dc, precision=exact)
    b = jnp.einsum("talr,hrqk->thaqlk", pick_dr, by_col, precision=exact)
    ok = row_ok[:, None, :, None, :, None] & col_ok[None, None, None, :, None, :]
    b = jnp.where(ok, b, NEG_INF)
    return b.reshape(3, rpb.shape[0], Q_TOK, K_BLOCKS * Q_TOK)


def _attn_kernel(q_ref, k0_ref, k1_ref, k2_ref, v0_ref, v1_ref, v2_ref, bias_ref, o_ref, *, n_heads):
    k_refs = (k0_ref, k1_ref, k2_ref)
    v_refs = (v0_ref, v1_ref, v2_ref)
    scale = HEAD_DIM ** -0.5
    for h in range(n_heads):
        hs = slice(h * HEAD_DIM, (h + 1) * HEAD_DIM)
        q = (q_ref[:, hs].astype(F32) * scale).astype(BF16)
        s = [lax.dot_general(q, k_refs[i][:, hs], (((1,), (1,)), ((), ())),
                             preferred_element_type=F32)
             + bias_ref[h, :, i * Q_TOK:(i + 1) * Q_TOK] for i in range(K_BLOCKS)]
        lane_tiles = lambda a: [a[:, c:c + HEAD_DIM] for c in range(0, Q_TOK, HEAD_DIM)]
        m = functools.reduce(jnp.maximum, [t for si in s for t in lane_tiles(si)])
        m = m.max(axis=-1, keepdims=True)
        p = [jnp.exp(si - m) for si in s]
        l = functools.reduce(jnp.add, [t for pi in p for t in lane_tiles(pi)])
        l = l.sum(axis=-1, keepdims=True)
        acc = jnp.dot(p[0].astype(BF16), v_refs[0][:, hs], preferred_element_type=F32)
        for i in range(1, K_BLOCKS):
            acc = acc + jnp.dot(p[i].astype(BF16), v_refs[i][:, hs], preferred_element_type=F32)
        o_ref[:, hs] = (acc * (1.0 / l)).astype(o_ref.dtype)


def _attention(u, bias, *, batch, seq, d_attn):
    t = u.shape[0]
    n_heads = d_attn // HEAD_DIM
    nblk = seq // Q_TOK
    assert nblk >= K_BLOCKS

    def q_map(b, j):
        return (b * nblk + j, 0)

    def kv_map(i, col):
        def f(b, j):
            return (b * nblk + jnp.clip(j - 1, 0, nblk - K_BLOCKS) + i, col)
        return f

    def bias_map(b, j):
        return (jnp.where(j == 0, 0, jnp.where(j == nblk - 1, 2, 1)), 0, 0, 0)

    blk = pl.BlockSpec((Q_TOK, d_attn), q_map)
    in_specs = [blk]
    in_specs += [pl.BlockSpec((Q_TOK, d_attn), kv_map(i, 1)) for i in range(K_BLOCKS)]
    in_specs += [pl.BlockSpec((Q_TOK, d_attn), kv_map(i, 2)) for i in range(K_BLOCKS)]
    in_specs += [pl.BlockSpec((None, n_heads, Q_TOK, K_BLOCKS * Q_TOK), bias_map)]
    est = 2 * 8 * Q_TOK * d_attn * 2 + 2 * n_heads * Q_TOK * K_BLOCKS * Q_TOK * 4
    return pl.pallas_call(
        functools.partial(_attn_kernel, n_heads=n_heads),
        out_shape=jax.ShapeDtypeStruct((t, d_attn), BF16),
        grid=(batch, nblk),
        in_specs=in_specs,
        out_specs=blk,
        compiler_params=_params(("parallel", "arbitrary"), est),
        name="nattn",
    )(u, u, u, u, u, u, u, bias)


def _scan_pitch(n_chunks):
    groups = n_chunks // V7X_SUBLANES
    return (groups + 1 - groups % 2) * V7X_SUBLANES


def _sigmoid(x):
    return 0.5 * jnp.tanh(0.5 * x) + 0.5


def _rglru_kernel(xr_ref, yg_ref, cw_ref, cb_ref, wg_ref, bg_ref, lam_ref, o_ref,
                  xp_ref, a_ref, b_ref, h_ref, cr_ref, *, seq):
    n_chunks = seq // SCAN_CHUNK
    pitch = _scan_pitch(n_chunks)
    w = REC_BLOCK_W
    halo = V7X_BF16_ROWS
    win_rows = SCAN_CHUNK + 2 * halo

    zeros = jnp.zeros((halo, w), xp_ref.dtype)
    xp_ref[0:halo, :] = zeros
    xp_ref[seq + halo:seq + 2 * halo, :] = zeros

    def stage(c, carry):
        r = pl.multiple_of(c * SCAN_CHUNK, SCAN_CHUNK)
        xp_ref[pl.ds(r + halo, SCAN_CHUNK), :] = xr_ref[pl.ds(r, SCAN_CHUNK), :]
        return carry
    lax.fori_loop(0, n_chunks, stage, 0)

    z = -lam_ref[...]
    coef = -C_RG * (jnp.maximum(z, 0.0) + jnp.log1p(jnp.exp(-jnp.abs(z))))

    left = CONV_W // 2

    def gates(c, carry):
        r = pl.multiple_of(c * SCAN_CHUNK, SCAN_CHUNK)
        win = xp_ref[pl.ds(r, win_rows), :].astype(F32)
        xc = cb_ref[...]
        for j in range(CONV_W):
            back = left - j
            tap = win if back == 0 else pltpu.roll(win, back % win_rows, 0)
            xc = xc + tap[halo:halo + SCAN_CHUNK] * cw_ref[j:j + 1, :]
        g = jnp.dot(xc.astype(BF16), wg_ref[...], preferred_element_type=F32) + bg_ref[...]
        g = _sigmoid(g)
        chunk_rows = pl.ds(c, SCAN_CHUNK, stride=pitch)
        for d in range(2):
            a = jnp.exp(g[:, d * w:(d + 1) * w] * coef[d:d + 1, :])
            y = 1.0 - a * a
            root = jnp.where(y > 0.0, y * lax.rsqrt(y), 0.0)
            a_ref[d, chunk_rows, :] = a
            b_ref[d, chunk_rows, :] = root * g[:, (2 + d) * w:(3 + d) * w] * xc
        return carry
    lax.fori_loop(0, n_chunks, gates, 0)

    def step_rows(d, t):
        pos = t if d == 0 else SCAN_CHUNK - 1 - t
        return pl.ds(pl.multiple_of(pos * pitch, V7X_SUBLANES), n_chunks)

    def pass1(t, carry):
        hs, ps = carry
        a = [a_ref[d, step_rows(d, t), :] for d in range(2)]
        return (tuple(a[d] * hs[d] + b_ref[d, step_rows(d, t), :] for d in range(2)),
                tuple(a[d] * ps[d] for d in range(2)))

    zero = jnp.zeros((n_chunks, w), F32)
    he, pe = lax.fori_loop(0, SCAN_CHUNK, pass1, ((zero, zero), (zero + 1.0, zero + 1.0)),
                           unroll=SCAN_UNROLL)

    for d in range(2):
        order = range(n_chunks) if d == 0 else range(n_chunks - 1, -1, -1)
        carry = jnp.zeros((1, w), F32)
        for j in order:
            cr_ref[d, j:j + 1, :] = carry
            carry = he[d][j:j + 1, :] + pe[d][j:j + 1, :] * carry

    def pass2(t, hs):
        nh = []
        for d in range(2):
            h = a_ref[d, step_rows(d, t), :] * hs[d] + b_ref[d, step_rows(d, t), :]
            h_ref[d, step_rows(d, t), :] = h
            nh.append(h)
        return tuple(nh)

    lax.fori_loop(0, SCAN_CHUNK, pass2, (cr_ref[0], cr_ref[1]), unroll=SCAN_UNROLL)

    def combine(c, carry):
        r = pl.multiple_of(c * SCAN_CHUNK, SCAN_CHUNK)
        chunk_rows = pl.ds(c, SCAN_CHUNK, stride=pitch)
        h = h_ref[0, chunk_rows, :] + h_ref[1, chunk_rows, :]
        y = yg_ref[pl.ds(r, SCAN_CHUNK), :].astype(F32)
        o_ref[pl.ds(r, SCAN_CHUNK), :] = (h * jax.nn.gelu(y)).astype(o_ref.dtype)
        return carry
    lax.fori_loop(0, n_chunks, combine, 0)


def _rglru(u3, conv_w, conv_b, wg, bg, lam, *, d_attn, d_rec):
    b, s, _ = u3.shape
    w = REC_BLOCK_W
    nb = d_rec // w
    assert s % (SCAN_CHUNK * V7X_SUBLANES) == 0
    n_chunks = s // SCAN_CHUNK
    x_col0 = 3 * d_attn // w
    y_col0 = x_col0 + nb
    scan_buf = pltpu.VMEM((2, SCAN_CHUNK * _scan_pitch(n_chunks), w), F32)
    est = (3 * 2 * s * w * 2 + (s + 2 * V7X_BF16_ROWS) * w * 2
           + 3 * 2 * SCAN_CHUNK * _scan_pitch(n_chunks) * w * 4)
    return pl.pallas_call(
        functools.partial(_rglru_kernel, seq=s),
        out_shape=jax.ShapeDtypeStruct((b, s, d_rec), BF16),
        grid=(b, nb),
        in_specs=[pl.BlockSpec((None, s, w), lambda i, n: (i, 0, x_col0 + n)),
                  pl.BlockSpec((None, s, w), lambda i, n: (i, 0, y_col0 + n)),
                  pl.BlockSpec((CONV_W, w), lambda i, n: (0, n)),
                  pl.BlockSpec((1, w), lambda i, n: (0, n)),
                  pl.BlockSpec((None, w, 4 * w), lambda i, n: (n, 0, 0)),
                  pl.BlockSpec((None, 1, 4 * w), lambda i, n: (n, 0, 0)),
                  pl.BlockSpec((2, w), lambda i, n: (0, n))],
        out_specs=pl.BlockSpec((None, s, w), lambda i, n: (i, 0, n)),
        scratch_shapes=[pltpu.VMEM((s + 2 * V7X_BF16_ROWS, w), BF16),
                        scan_buf, scan_buf, scan_buf, pltpu.VMEM((2, n_chunks, w), F32)],
        compiler_params=_params(("parallel", "arbitrary"), est),
        name="rglru",
    )(u3, u3, conv_w, conv_b, wg, bg, lam)


def _outproj_kernel(a_ref, r_ref, x_ref, ga_ref, gr_ref, w_ref, gp_ref, o_ref, cat_ref):
    da = a_ref.shape[1]
    cat_ref[:, :da] = _rms(a_ref[...].astype(F32), ga_ref[...]).astype(BF16)
    cat_ref[:, da:] = _rms(r_ref[...].astype(F32), gr_ref[...]).astype(BF16)
    mixed = jnp.dot(cat_ref[...], w_ref[...], preferred_element_type=F32)
    o_ref[...] = x_ref[...] + _rms(mixed, gp_ref[...])


def _out_proj(attn, rec, x2, g_attn, g_rec, w, g_post, *, tm):
    t, d = x2.shape
    da, dr = attn.shape[1], rec.shape[1]
    est = 2 * tm * (da + dr) * 2 + 4 * tm * d * 4 + (da + dr) * d * 2 + tm * (da + dr) * 2
    row = lambda i: (i, 0)
    fixed = lambda i: (0, 0)
    return pl.pallas_call(
        _outproj_kernel,
        out_shape=jax.ShapeDtypeStruct((t, d), F32),
        grid=(t // tm,),
        in_specs=[pl.BlockSpec((tm, da), row), pl.BlockSpec((tm, dr), row), pl.BlockSpec((tm, d), row),
                  pl.BlockSpec((1, da), fixed), pl.BlockSpec((1, dr), fixed),
                  pl.BlockSpec((da + dr, d), fixed, pipeline_mode=pl.Buffered(1)),
                  pl.BlockSpec((1, d), fixed)],
        out_specs=pl.BlockSpec((tm, d), row),
        scratch_shapes=[pltpu.VMEM((tm, da + dr), BF16)],
        compiler_params=_params(("parallel",), est),
        name="out_proj",
    )(attn, rec, x2, g_attn, g_rec, w, g_post)


def _ffn_kernel(h_ref, gpre_ref, wg_ref, wu_ref, wd_ref, gpost_ref, o_ref, fn_ref, acc_ref, *, row_chunk):
    f = pl.program_id(1)

    @pl.when(f == 0)
    def _():
        def body(c, carry):
            r = pl.multiple_of(c * row_chunk, row_chunk)
            fn_ref[pl.ds(r, row_chunk), :] = _rms(h_ref[pl.ds(r, row_chunk), :], gpre_ref[...]).astype(BF16)
            return carry
        lax.fori_loop(0, h_ref.shape[0] // row_chunk, body, 0)

    fn = fn_ref[...]
    gate = jnp.dot(fn, wg_ref[...], preferred_element_type=F32)
    up = jnp.dot(fn, wu_ref[...], preferred_element_type=F32)
    act = (jax.nn.silu(gate) * up).astype(BF16)
    @pl.when(f == 0)
    def _():
        acc_ref[...] = jnp.dot(act, wd_ref[...], preferred_element_type=F32)

    @pl.when(f > 0)
    def _():
        acc_ref[...] += jnp.dot(act, wd_ref[...], preferred_element_type=F32)

    @pl.when(f == pl.num_programs(1) - 1)
    def _():
        def body(c, carry):
            r = pl.multiple_of(c * row_chunk, row_chunk)
            o_ref[pl.ds(r, row_chunk), :] = _rms(acc_ref[pl.ds(r, row_chunk), :], gpost_ref[...]).astype(o_ref.dtype)
            return carry
        lax.fori_loop(0, h_ref.shape[0] // row_chunk, body, 0)


def _ffn(h1, g_pre, w_gate, w_up, w_down, g_post, *, tm, tf):
    t, d = h1.shape
    dff = w_gate.shape[1]
    est = 2 * tm * d * 4 + 3 * 2 * d * tf * 2 + 2 * tm * d * 2 + tm * d * 2 + tm * d * 4
    return pl.pallas_call(
        functools.partial(_ffn_kernel, row_chunk=min(128, tm)),
        out_shape=jax.ShapeDtypeStruct((t, d), BF16),
        grid=(t // tm, dff // tf),
        in_specs=[pl.BlockSpec((tm, d), lambda i, f: (i, 0)),
                  pl.BlockSpec((1, d), lambda i, f: (0, 0)),
                  pl.BlockSpec((d, tf), lambda i, f: (0, f)),
                  pl.BlockSpec((d, tf), lambda i, f: (0, f)),
                  pl.BlockSpec((tf, d), lambda i, f: (f, 0)),
                  pl.BlockSpec((1, d), lambda i, f: (0, 0))],
        out_specs=pl.BlockSpec((tm, d), lambda i, f: (i, 0)),
        scratch_shapes=[pltpu.VMEM((tm, d), BF16), pltpu.VMEM((tm, d), F32)],
        compiler_params=_params(("parallel", "arbitrary"), est),
        name="ffn",
    )(h1, g_pre, w_gate, w_up, w_down, g_post)


def _ple_kernel(h_ref, d_ref, p_ref, gpre_ref, wg_ref, wp_ref, gpost_ref, o_ref):
    h2 = h_ref[...] + d_ref[...].astype(F32)
    gate = jax.nn.sigmoid(jnp.dot(_rms(h2, gpre_ref[...]).astype(BF16), wg_ref[...],
                                  preferred_element_type=F32))
    ple = jnp.dot(p_ref[...].astype(BF16), wp_ref[...], preferred_element_type=F32)
    o_ref[...] = h2 + _rms(gate * ple, gpost_ref[...])


def _ple(h1, dff, p2, g_pre, w_gate, w_proj, g_post, *, tm):
    t, d = h1.shape
    dp = p2.shape[1]
    est = 4 * tm * d * 4 + 2 * tm * d * 2 + 2 * tm * dp * 4 + d * d * 2 + 2 * dp * d * 2 + 3 * tm * d * 4
    row = lambda i: (i, 0)
    fixed = lambda i: (0, 0)
    return pl.pallas_call(
        _ple_kernel,
        out_shape=jax.ShapeDtypeStruct((t, d), F32),
        grid=(t // tm,),
        in_specs=[pl.BlockSpec((tm, d), row), pl.BlockSpec((tm, d), row), pl.BlockSpec((tm, dp), row),
                  pl.BlockSpec((1, d), fixed),
                  pl.BlockSpec((d, d), fixed, pipeline_mode=pl.Buffered(1)),
                  pl.BlockSpec((dp, d), fixed),
                  pl.BlockSpec((1, d), fixed)],
        out_specs=pl.BlockSpec((tm, d), row),
        compiler_params=_params(("parallel",), est),
        name="ple",
    )(h1, dff, p2, g_pre, w_gate, w_proj, g_post)


def _layer(h, p_i, g_mix_pre, w_in, rpb, conv_w, conv_b, w_rg_a, b_rg_a, w_rg_i, b_rg_i, lam,
           g_attn_out, g_rec_out, w_out, g_mix_post, g_ffn_pre, w_ffn_gate, w_ffn_up, w_ffn_down,
           g_ffn_post, g_ple_pre, w_ple_gate, w_ple_proj, g_ple_post):
    batch, seq, d = h.shape
    t = batch * seq
    d_attn = g_attn_out.shape[0]
    d_rec = g_rec_out.shape[0]
    nb = d_rec // REC_BLOCK_W
    row = lambda v: v.reshape(1, -1).astype(F32)

    x2 = h.reshape(t, d)
    u = _in_proj(x2, row(g_mix_pre), w_in.astype(BF16), tm=min(1024, t), tn=1024)

    attn = _attention(u, _attn_bias(rpb, seq // GRID_W), batch=batch, seq=seq, d_attn=d_attn)

    wg = jnp.concatenate([w_rg_a[0], w_rg_a[1], w_rg_i[0], w_rg_i[1]], axis=-1).astype(BF16)
    bg = jnp.concatenate([b_rg_a.reshape(2, nb, 1, REC_BLOCK_W)[0], b_rg_a.reshape(2, nb, 1, REC_BLOCK_W)[1],
                          b_rg_i.reshape(2, nb, 1, REC_BLOCK_W)[0], b_rg_i.reshape(2, nb, 1, REC_BLOCK_W)[1]],
                         axis=-1).astype(F32)
    rec = _rglru(u.reshape(batch, seq, -1), conv_w.astype(F32), row(conv_b), wg, bg, lam.astype(F32),
                 d_attn=d_attn, d_rec=d_rec)

    h1 = _out_proj(attn, rec.reshape(t, d_rec), x2, row(g_attn_out), row(g_rec_out),
                   w_out.astype(BF16), row(g_mix_post), tm=min(512, t))

    dff = _ffn(h1, row(g_ffn_pre), w_ffn_gate.astype(BF16), w_ffn_up.astype(BF16),
               w_ffn_down.astype(BF16), row(g_ffn_post), tm=min(1024, t), tf=512)

    out = _ple(h1, dff, p_i.reshape(t, -1), row(g_ple_pre), w_ple_gate.astype(BF16),
               w_ple_proj.astype(BF16), row(g_ple_post), tm=min(512, t))
    return out.reshape(batch, seq, d)


def kernel(x, p, g_mix_pre, w_in, rpb, conv_w, conv_b, w_rg_a, b_rg_a, w_rg_i, b_rg_i, lam, g_attn_out, g_rec_out, w_out, g_mix_post, g_ffn_pre, w_ffn_gate, w_ffn_up, w_ffn_down, g_ffn_post, g_ple_pre, w_ple_gate, w_ple_proj, g_ple_post):
    h = x
    for i in range(p.shape[0]):
        h = _layer(h, p[i], g_mix_pre[i], w_in[i], rpb[i], conv_w[i], conv_b[i], w_rg_a[i], b_rg_a[i],
                   w_rg_i[i], b_rg_i[i], lam[i], g_attn_out[i], g_rec_out[i], w_out[i], g_mix_post[i],
                   g_ffn_pre[i], w_ffn_gate[i], w_ffn_up[i], w_ffn_down[i], g_ffn_post[i], g_ple_pre[i],
                   w_ple_gate[i], w_ple_proj[i], g_ple_post[i])
    return h
```

```python
import functools

import numpy as np
import jax
import jax.numpy as jnp
from jax import lax
from jax.experimental import pallas as pl
from jax.experimental.pallas import tpu as pltpu

F32 = jnp.float32
BF16 = jnp.bfloat16

EPS = 1e-6
NEG_INF = -1e9
GRID_W = 64
HEAD_DIM = 128
WIN_R = 8
WIN_C = 16
CONV_W = 4
C_RG = 8.0
REC_BLOCK_W = 128

V7X_SUBLANES = 8
V7X_BF16_ROWS = 16
V7X_VMEM_BYTES = 64 * 1024 * 1024
V7X_VMEM_CAP = V7X_VMEM_BYTES - 8 * 1024 * 1024

Q_ROWS = 4
Q_TOK = Q_ROWS * GRID_W
K_BLOCKS = 3

SCAN_CHUNK = 256
SCAN_UNROLL = 8


def _vmem_limit(estimate_bytes):
    return int(min(estimate_bytes * 5 // 4 + (4 << 20), V7X_VMEM_CAP))


def _params(semantics, estimate_bytes):
    return pltpu.CompilerParams(dimension_semantics=semantics,
                                vmem_limit_bytes=_vmem_limit(estimate_bytes))


def _rms(x, g):
    ms = jnp.mean(x * x, axis=-1, keepdims=True)
    return x * lax.rsqrt(ms + EPS) * g


def _inproj_kernel(x_ref, g_ref, w_ref, o_ref, hn_ref, *, tn):
    hn_ref[...] = _rms(x_ref[...], g_ref[...]).astype(BF16)
    for c in range(0, o_ref.shape[1], tn):
        o_ref[:, c:c + tn] = jnp.dot(hn_ref[...], w_ref[:, c:c + tn],
                                     preferred_element_type=F32).astype(o_ref.dtype)


def _in_proj(x2, g, w, *, tm, tn):
    t, d = x2.shape
    n = w.shape[1]
    est = 2 * tm * d * 4 + d * n * 2 + 2 * tm * n * 2 + tm * d * 2 + tm * d * 4
    return pl.pallas_call(
        functools.partial(_inproj_kernel, tn=tn),
        out_shape=jax.ShapeDtypeStruct((t, n), BF16),
        grid=(t // tm,),
        in_specs=[pl.BlockSpec((tm, d), lambda i: (i, 0)),
                  pl.BlockSpec((1, d), lambda i: (0, 0)),
                  pl.BlockSpec((d, n), lambda i: (0, 0), pipeline_mode=pl.Buffered(1))],
        out_specs=pl.BlockSpec((tm, n), lambda i: (i, 0)),
        scratch_shapes=[pltpu.VMEM((tm, d), BF16)],
        compiler_params=_params(("parallel",), est),
        name="in_proj",
    )(x2, g, w)


def _attn_bias(rpb, rows):
    nblk = rows // Q_ROWS
    kr = min(WIN_R, rows)
    n_dr, n_dc = 2 * WIN_R - 1, 2 * WIN_C - 1
    q_row0 = np.array([0, Q_ROWS, rows - Q_ROWS])
    k_row0 = np.array([0, 0, (nblk - K_BLOCKS) * Q_ROWS])
    qr = q_row0[:, None] + np.arange(Q_ROWS)[None, :]
    kro = k_row0[:, None] + np.arange(K_BLOCKS * Q_ROWS)[None, :]
    rstart = np.clip(qr - kr // 2, 0, rows - kr)
    row_ok = (kro[:, None, :] >= rstart[:, :, None]) & (kro[:, None, :] < rstart[:, :, None] + kr)
    dr = np.clip(kro[:, None, :] - qr[:, :, None] + WIN_R - 1, 0, n_dr - 1)
    cols = np.arange(GRID_W)
    cstart = np.clip(cols - WIN_C // 2, 0, GRID_W - WIN_C)
    col_ok = (cols[None, :] >= cstart[:, None]) & (cols[None, :] < cstart[:, None] + WIN_C)
    dc = np.clip(cols[None, :] - cols[:, None] + WIN_C - 1, 0, n_dc - 1)
    pick_dc = (dc[None] == np.arange(n_dc)[:, None, None]).astype(np.float32)
    pick_dr = (dr[..., None] == np.arange(n_dr)).astype(np.float32)
    exact = lax.Precision.HIGHEST
    by_col = jnp.einsum("hrd,dqk->hrqk", rpb.astype(F32), pick_dc, precision=exact)
    b = jnp.einsum("talr,hrqk->thaqlk", pick_dr, by_col, precision=exact)
    ok = row_ok[:, None, :, None, :, None] & col_ok[None, None, None, :, None, :]
    b = jnp.where(ok, b, NEG_INF)
    return b.reshape(3, rpb.shape[0], Q_TOK, K_BLOCKS * Q_TOK)


def _attn_kernel(q_ref, k0_ref, k1_ref, k2_ref, v0_ref, v1_ref, v2_ref, bias_ref, o_ref, *, n_heads):
    k_refs = (k0_ref, k1_ref, k2_ref)
    v_refs = (v0_ref, v1_ref, v2_ref)
    scale = HEAD_DIM ** -0.5
    for h in range(n_heads):
        hs = slice(h * HEAD_DIM, (h + 1) * HEAD_DIM)
        q = (q_ref[:, hs].astype(F32) * scale).astype(BF16)
        s = [lax.dot_general(q, k_refs[i][:, hs], (((1,), (1,)), ((), ())),
                             preferred_element_type=F32)
             + bias_ref[h, :, i * Q_TOK:(i + 1) * Q_TOK] for i in range(K_BLOCKS)]
        lane_tiles = lambda a: [a[:, c:c + HEAD_DIM] for c in range(0, Q_TOK, HEAD_DIM)]
        m = functools.reduce(jnp.maximum, [t for si in s for t in lane_tiles(si)])
        m = m.max(axis=-1, keepdims=True)
        p = [jnp.exp(si - m) for si in s]
        l = functools.reduce(jnp.add, [t for pi in p for t in lane_tiles(pi)])
        l = l.sum(axis=-1, keepdims=True)
        acc = jnp.dot(p[0].astype(BF16), v_refs[0][:, hs], preferred_element_type=F32)
        for i in range(1, K_BLOCKS):
            acc = acc + jnp.dot(p[i].astype(BF16), v_refs[i][:, hs], preferred_element_type=F32)
        o_ref[:, hs] = (acc * (1.0 / l)).astype(o_ref.dtype)


def _attention(u, bias, *, batch, seq, d_attn):
    t = u.shape[0]
    n_heads = d_attn // HEAD_DIM
    nblk = seq // Q_TOK
    assert nblk >= K_BLOCKS

    def q_map(b, j):
        return (b * nblk + j, 0)

    def kv_map(i, col):
        def f(b, j):
            return (b * nblk + jnp.clip(j - 1, 0, nblk - K_BLOCKS) + i, col)
        return f

    def bias_map(b, j):
        return (jnp.where(j == 0, 0, jnp.where(j == nblk - 1, 2, 1)), 0, 0, 0)

    blk = pl.BlockSpec((Q_TOK, d_attn), q_map)
    in_specs = [blk]
    in_specs += [pl.BlockSpec((Q_TOK, d_attn), kv_map(i, 1)) for i in range(K_BLOCKS)]
    in_specs += [pl.BlockSpec((Q_TOK, d_attn), kv_map(i, 2)) for i in range(K_BLOCKS)]
    in_specs += [pl.BlockSpec((None, n_heads, Q_TOK, K_BLOCKS * Q_TOK), bias_map)]
    est = 2 * 8 * Q_TOK * d_attn * 2 + 2 * n_heads * Q_TOK * K_BLOCKS * Q_TOK * 4
    return pl.pallas_call(
        functools.partial(_attn_kernel, n_heads=n_heads),
        out_shape=jax.ShapeDtypeStruct((t, d_attn), BF16),
        grid=(batch, nblk),
        in_specs=in_specs,
        out_specs=blk,
        compiler_params=_params(("parallel", "arbitrary"), est),
        name="nattn",
    )(u, u, u, u, u, u, u, bias)


def _scan_pitch(n_chunks):
    groups = n_chunks // V7X_SUBLANES
    return (groups + 1 - groups % 2) * V7X_SUBLANES


def _sigmoid(x):
    return 0.5 * jnp.tanh(0.5 * x) + 0.5


def _rglru_kernel(xr_ref, yg_ref, cw_ref, cb_ref, wg_ref, bg_ref, lam_ref, o_ref,
                  xp_ref, a_ref, b_ref, h_ref, cr_ref, *, seq):
    n_chunks = seq // SCAN_CHUNK
    pitch = _scan_pitch(n_chunks)
    w = REC_BLOCK_W
    halo = V7X_BF16_ROWS
    win_rows = SCAN_CHUNK + 2 * halo

    zeros = jnp.zeros((halo, w), xp_ref.dtype)
    xp_ref[0:halo, :] = zeros
    xp_ref[seq + halo:seq + 2 * halo, :] = zeros

    def stage(c, carry):
        r = pl.multiple_of(c * SCAN_CHUNK, SCAN_CHUNK)
        xp_ref[pl.ds(r + halo, SCAN_CHUNK), :] = xr_ref[pl.ds(r, SCAN_CHUNK), :]
        return carry
    lax.fori_loop(0, n_chunks, stage, 0)

    z = -lam_ref[...]
    half_coef = (-0.5 * C_RG) * (jnp.maximum(z, 0.0) + jnp.log1p(jnp.exp(-jnp.abs(z))))

    left = CONV_W // 2

    def gates(c, carry):
        r = pl.multiple_of(c * SCAN_CHUNK, SCAN_CHUNK)
        win = xp_ref[pl.ds(r, win_rows), :].astype(F32)
        xc = cb_ref[...]
        for j in range(CONV_W):
            back = left - j
            tap = win if back == 0 else pltpu.roll(win, back % win_rows, 0)
            xc = xc + tap[halo:halo + SCAN_CHUNK] * cw_ref[j:j + 1, :]
        th = jnp.tanh(jnp.dot(xc.astype(BF16), wg_ref[...], preferred_element_type=F32) + bg_ref[...])
        xh = 0.5 * xc
        chunk_rows = pl.ds(c, SCAN_CHUNK, stride=pitch)
        for d in range(2):
            a = jnp.exp(th[:, d * w:(d + 1) * w] * half_coef[d:d + 1, :] + half_coef[d:d + 1, :])
            y = 1.0 - a * a
            root = jnp.where(y > 0.0, y * lax.rsqrt(y), 0.0)
            a_ref[d, chunk_rows, :] = a
            b_ref[d, chunk_rows, :] = root * (th[:, (2 + d) * w:(3 + d) * w] + 1.0) * xh
        return carry
    lax.fori_loop(0, n_chunks, gates, 0)

    def step_rows(d, t):
        pos = t if d == 0 else SCAN_CHUNK - 1 - t
        return pl.ds(pl.multiple_of(pos * pitch, V7X_SUBLANES), n_chunks)

    def pass1(t, carry):
        hs, ps = carry
        a = [a_ref[d, step_rows(d, t), :] for d in range(2)]
        return (tuple(a[d] * hs[d] + b_ref[d, step_rows(d, t), :] for d in range(2)),
                tuple(a[d] * ps[d] for d in range(2)))

    zero = jnp.zeros((n_chunks, w), F32)
    he, pe = lax.fori_loop(0, SCAN_CHUNK, pass1, ((zero, zero), (zero + 1.0, zero + 1.0)),
                           unroll=SCAN_UNROLL)

    for d in range(2):
        order = range(n_chunks) if d == 0 else range(n_chunks - 1, -1, -1)
        carry = jnp.zeros((1, w), F32)
        for j in order:
            cr_ref[d, j:j + 1, :] = carry
            carry = he[d][j:j + 1, :] + pe[d][j:j + 1, :] * carry

    def pass2(t, hs):
        nh = []
        for d in range(2):
            h = a_ref[d, step_rows(d, t), :] * hs[d] + b_ref[d, step_rows(d, t), :]
            h_ref[d, step_rows(d, t), :] = h
            nh.append(h)
        return tuple(nh)

    lax.fori_loop(0, SCAN_CHUNK, pass2, (cr_ref[0], cr_ref[1]), unroll=SCAN_UNROLL)

    def combine(c, carry):
        r = pl.multiple_of(c * SCAN_CHUNK, SCAN_CHUNK)
        chunk_rows = pl.ds(c, SCAN_CHUNK, stride=pitch)
        h = h_ref[0, chunk_rows, :] + h_ref[1, chunk_rows, :]
        y = yg_ref[pl.ds(r, SCAN_CHUNK), :].astype(F32)
        o_ref[pl.ds(r, SCAN_CHUNK), :] = (h * jax.nn.gelu(y)).astype(o_ref.dtype)
        return carry
    lax.fori_loop(0, n_chunks, combine, 0)


def _rglru(u3, conv_w, conv_b, wg, bg, lam, *, d_attn, d_rec):
    b, s, _ = u3.shape
    w = REC_BLOCK_W
    nb = d_rec // w
    assert s % (SCAN_CHUNK * V7X_SUBLANES) == 0
    n_chunks = s // SCAN_CHUNK
    x_col0 = 3 * d_attn // w
    y_col0 = x_col0 + nb
    scan_buf = pltpu.VMEM((2, SCAN_CHUNK * _scan_pitch(n_chunks), w), F32)
    est = (3 * 2 * s * w * 2 + (s + 2 * V7X_BF16_ROWS) * w * 2
           + 3 * 2 * SCAN_CHUNK * _scan_pitch(n_chunks) * w * 4)
    return pl.pallas_call(
        functools.partial(_rglru_kernel, seq=s),
        out_shape=jax.ShapeDtypeStruct((b, s, d_rec), BF16),
        grid=(b, nb),
        in_specs=[pl.BlockSpec((None, s, w), lambda i, n: (i, 0, x_col0 + n)),
                  pl.BlockSpec((None, s, w), lambda i, n: (i, 0, y_col0 + n)),
                  pl.BlockSpec((CONV_W, w), lambda i, n: (0, n)),
                  pl.BlockSpec((1, w), lambda i, n: (0, n)),
                  pl.BlockSpec((None, w, 4 * w), lambda i, n: (n, 0, 0)),
                  pl.BlockSpec((None, 1, 4 * w), lambda i, n: (n, 0, 0)),
                  pl.BlockSpec((2, w), lambda i, n: (0, n))],
        out_specs=pl.BlockSpec((None, s, w), lambda i, n: (i, 0, n)),
        scratch_shapes=[pltpu.VMEM((s + 2 * V7X_BF16_ROWS, w), BF16),
                        scan_buf, scan_buf, scan_buf, pltpu.VMEM((2, n_chunks, w), F32)],
        compiler_params=_params(("parallel", "arbitrary"), est),
        name="rglru",
    )(u3, u3, conv_w, conv_b, wg, bg, lam)


def _outproj_kernel(a_ref, r_ref, x_ref, ga_ref, gr_ref, w_ref, gp_ref, o_ref, cat_ref):
    da = a_ref.shape[1]
    cat_ref[:, :da] = _rms(a_ref[...].astype(F32), ga_ref[...]).astype(BF16)
    cat_ref[:, da:] = _rms(r_ref[...].astype(F32), gr_ref[...]).astype(BF16)
    mixed = jnp.dot(cat_ref[...], w_ref[...], preferred_element_type=F32)
    o_ref[...] = x_ref[...] + _rms(mixed, gp_ref[...])


def _out_proj(attn, rec, x2, g_attn, g_rec, w, g_post, *, tm):
    t, d = x2.shape
    da, dr = attn.shape[1], rec.shape[1]
    est = 2 * tm * (da + dr) * 2 + 4 * tm * d * 4 + (da + dr) * d * 2 + tm * (da + dr) * 2
    row = lambda i: (i, 0)
    fixed = lambda i: (0, 0)
    return pl.pallas_call(
        _outproj_kernel,
        out_shape=jax.ShapeDtypeStruct((t, d), F32),
        grid=(t // tm,),
        in_specs=[pl.BlockSpec((tm, da), row), pl.BlockSpec((tm, dr), row), pl.BlockSpec((tm, d), row),
                  pl.BlockSpec((1, da), fixed), pl.BlockSpec((1, dr), fixed),
                  pl.BlockSpec((da + dr, d), fixed, pipeline_mode=pl.Buffered(1)),
                  pl.BlockSpec((1, d), fixed)],
        out_specs=pl.BlockSpec((tm, d), row),
        scratch_shapes=[pltpu.VMEM((tm, da + dr), BF16)],
        compiler_params=_params(("parallel",), est),
        name="out_proj",
    )(attn, rec, x2, g_attn, g_rec, w, g_post)


def _ffn_kernel(h_ref, gpre_ref, wg_ref, wu_ref, wd_ref, gpost_ref, o_ref, fn_ref, acc_ref, *, row_chunk):
    f = pl.program_id(1)

    @pl.when(f == 0)
    def _():
        def body(c, carry):
            r = pl.multiple_of(c * row_chunk, row_chunk)
            fn_ref[pl.ds(r, row_chunk), :] = _rms(h_ref[pl.ds(r, row_chunk), :], gpre_ref[...]).astype(BF16)
            return carry
        lax.fori_loop(0, h_ref.shape[0] // row_chunk, body, 0)

    fn = fn_ref[...]
    gate = jnp.dot(fn, wg_ref[...], preferred_element_type=F32)
    up = jnp.dot(fn, wu_ref[...], preferred_element_type=F32)
    act = (jax.nn.silu(gate) * up).astype(BF16)
    @pl.when(f == 0)
    def _():
        acc_ref[...] = jnp.dot(act, wd_ref[...], preferred_element_type=F32)

    @pl.when(f > 0)
    def _():
        acc_ref[...] += jnp.dot(act, wd_ref[...], preferred_element_type=F32)

    @pl.when(f == pl.num_programs(1) - 1)
    def _():
        def body(c, carry):
            r = pl.multiple_of(c * row_chunk, row_chunk)
            o_ref[pl.ds(r, row_chunk), :] = _rms(acc_ref[pl.ds(r, row_chunk), :], gpost_ref[...]).astype(o_ref.dtype)
            return carry
        lax.fori_loop(0, h_ref.shape[0] // row_chunk, body, 0)


def _ffn(h1, g_pre, w_gate, w_up, w_down, g_post, *, tm, tf):
    t, d = h1.shape
    dff = w_gate.shape[1]
    est = 2 * tm * d * 4 + 3 * 2 * d * tf * 2 + 2 * tm * d * 2 + tm * d * 2 + tm * d * 4
    return pl.pallas_call(
        functools.partial(_ffn_kernel, row_chunk=min(128, tm)),
        out_shape=jax.ShapeDtypeStruct((t, d), BF16),
        grid=(t // tm, dff // tf),
        in_specs=[pl.BlockSpec((tm, d), lambda i, f: (i, 0)),
                  pl.BlockSpec((1, d), lambda i, f: (0, 0)),
                  pl.BlockSpec((d, tf), lambda i, f: (0, f)),
                  pl.BlockSpec((d, tf), lambda i, f: (0, f)),
                  pl.BlockSpec((tf, d), lambda i, f: (f, 0)),
                  pl.BlockSpec((1, d), lambda i, f: (0, 0))],
        out_specs=pl.BlockSpec((tm, d), lambda i, f: (i, 0)),
        scratch_shapes=[pltpu.VMEM((tm, d), BF16), pltpu.VMEM((tm, d), F32)],
        compiler_params=_params(("parallel", "arbitrary"), est),
        name="ffn",
    )(h1, g_pre, w_gate, w_up, w_down, g_post)


def _ple_kernel(h_ref, d_ref, p_ref, gpre_ref, wg_ref, wp_ref, gpost_ref, o_ref):
    h2 = h_ref[...] + d_ref[...].astype(F32)
    gate = _sigmoid(jnp.dot(_rms(h2, gpre_ref[...]).astype(BF16), wg_ref[...],
                            preferred_element_type=F32))
    ple = jnp.dot(p_ref[...].astype(BF16), wp_ref[...], preferred_element_type=F32)
    o_ref[...] = h2 + _rms(gate * ple, gpost_ref[...])


def _ple(h1, dff, p2, g_pre, w_gate, w_proj, g_post, *, tm):
    t, d = h1.shape
    dp = p2.shape[1]
    est = 4 * tm * d * 4 + 2 * tm * d * 2 + 2 * tm * dp * 4 + d * d * 2 + 2 * dp * d * 2 + 3 * tm * d * 4
    row = lambda i: (i, 0)
    fixed = lambda i: (0, 0)
    return pl.pallas_call(
        _ple_kernel,
        out_shape=jax.ShapeDtypeStruct((t, d), F32),
        grid=(t // tm,),
        in_specs=[pl.BlockSpec((tm, d), row), pl.BlockSpec((tm, d), row), pl.BlockSpec((tm, dp), row),
                  pl.BlockSpec((1, d), fixed),
                  pl.BlockSpec((d, d), fixed, pipeline_mode=pl.Buffered(1)),
                  pl.BlockSpec((dp, d), fixed),
                  pl.BlockSpec((1, d), fixed)],
        out_specs=pl.BlockSpec((tm, d), row),
        compiler_params=_params(("parallel",), est),
        name="ple",
    )(h1, dff, p2, g_pre, w_gate, w_proj, g_post)


def _layer(h, p_i, g_mix_pre, w_in, rpb, conv_w, conv_b, w_rg_a, b_rg_a, w_rg_i, b_rg_i, lam,
           g_attn_out, g_rec_out, w_out, g_mix_post, g_ffn_pre, w_ffn_gate, w_ffn_up, w_ffn_down,
           g_ffn_post, g_ple_pre, w_ple_gate, w_ple_proj, g_ple_post):
    batch, seq, d = h.shape
    t = batch * seq
    d_attn = g_attn_out.shape[0]
    d_rec = g_rec_out.shape[0]
    nb = d_rec // REC_BLOCK_W
    row = lambda v: v.reshape(1, -1).astype(F32)

    x2 = h.reshape(t, d)
    u = _in_proj(x2, row(g_mix_pre), w_in.astype(BF16), tm=min(512, t), tn=1024)

    attn = _attention(u, _attn_bias(rpb, seq // GRID_W), batch=batch, seq=seq, d_attn=d_attn)

    wg = (0.5 * jnp.concatenate([w_rg_a[0], w_rg_a[1], w_rg_i[0], w_rg_i[1]], axis=-1)).astype(BF16)
    bg = jnp.concatenate([b_rg_a.reshape(2, nb, 1, REC_BLOCK_W)[0], b_rg_a.reshape(2, nb, 1, REC_BLOCK_W)[1],
                          b_rg_i.reshape(2, nb, 1, REC_BLOCK_W)[0], b_rg_i.reshape(2, nb, 1, REC_BLOCK_W)[1]],
                         axis=-1).astype(F32) * 0.5
    rec = _rglru(u.reshape(batch, seq, -1), conv_w.astype(F32), row(conv_b), wg, bg, lam.astype(F32),
                 d_attn=d_attn, d_rec=d_rec)

    h1 = _out_proj(attn, rec.reshape(t, d_rec), x2, row(g_attn_out), row(g_rec_out),
                   w_out.astype(BF16), row(g_mix_post), tm=min(512, t))

    dff = _ffn(h1, row(g_ffn_pre), w_ffn_gate.astype(BF16), w_ffn_up.astype(BF16),
               w_ffn_down.astype(BF16), row(g_ffn_post), tm=min(1024, t), tf=512)

    out = _ple(h1, dff, p_i.reshape(t, -1), row(g_ple_pre), w_ple_gate.astype(BF16),
               w_ple_proj.astype(BF16), row(g_ple_post), tm=min(512, t))
    return out.reshape(batch, seq, d)


def kernel(x, p, g_mix_pre, w_in, rpb, conv_w, conv_b, w_rg_a, b_rg_a, w_rg_i, b_rg_i, lam, g_attn_out, g_rec_out, w_out, g_mix_post, g_ffn_pre, w_ffn_gate, w_ffn_up, w_ffn_down, g_ffn_post, g_ple_pre, w_ple_gate, w_ple_proj, g_ple_post):
    h = x
    for i in range(p.shape[0]):
        h = _layer(h, p[i], g_mix_pre[i], w_in[i], rpb[i], conv_w[i], conv_b[i], w_rg_a[i], b_rg_a[i],
                   w_rg_i[i], b_rg_i[i], lam[i], g_attn_out[i], g_rec_out[i], w_out[i], g_mix_post[i],
                   g_ffn_pre[i], w_ffn_gate[i], w_ffn_up[i], w_ffn_down[i], g_ffn_post[i], g_ple_pre[i],
                   w_ple_gate[i], w_ple_proj[i], g_ple_post[i])
    return h
```

```python
import functools

import numpy as np
import jax
import jax.numpy as jnp
from jax import lax
from jax.experimental import pallas as pl
from jax.experimental.pallas import tpu as pltpu

F32 = jnp.float32
BF16 = jnp.bfloat16

EPS = 1e-6
NEG_INF = -1e9
GRID_W = 64
HEAD_DIM = 128
WIN_R = 8
WIN_C = 16
CONV_W = 4
C_RG = 8.0
REC_BLOCK_W = 128

V7X_SUBLANES = 8
V7X_BF16_ROWS = 16
V7X_VMEM_BYTES = 64 * 1024 * 1024
V7X_VMEM_CAP = V7X_VMEM_BYTES - 8 * 1024 * 1024

Q_ROWS = 4
Q_TOK = Q_ROWS * GRID_W
K_BLOCKS = 3

SCAN_CHUNK = 256
SCAN_UNROLL = 8


def _vmem_limit(estimate_bytes):
    return int(min(estimate_bytes * 5 // 4 + (4 << 20), V7X_VMEM_CAP))


def _params(semantics, estimate_bytes):
    return pltpu.CompilerParams(dimension_semantics=semantics,
                                vmem_limit_bytes=_vmem_limit(estimate_bytes))


def _rms(x, g):
    ms = jnp.mean(x * x, axis=-1, keepdims=True)
    return x * lax.rsqrt(ms + EPS) * g


def _inproj_kernel(x_ref, g_ref, w_ref, o_ref, hn_ref, *, tn):
    hn_ref[...] = _rms(x_ref[...], g_ref[...]).astype(BF16)
    for c in range(0, o_ref.shape[1], tn):
        o_ref[:, c:c + tn] = jnp.dot(hn_ref[...], w_ref[:, c:c + tn],
                                     preferred_element_type=F32).astype(o_ref.dtype)


def _in_proj(x2, g, w, *, tm, tn):
    t, d = x2.shape
    n = w.shape[1]
    est = 2 * tm * d * 4 + d * n * 2 + 2 * tm * n * 2 + tm * d * 2 + tm * d * 4
    return pl.pallas_call(
        functools.partial(_inproj_kernel, tn=tn),
        out_shape=jax.ShapeDtypeStruct((t, n), BF16),
        grid=(t // tm,),
        in_specs=[pl.BlockSpec((tm, d), lambda i: (i, 0)),
                  pl.BlockSpec((1, d), lambda i: (0, 0)),
                  pl.BlockSpec((d, n), lambda i: (0, 0), pipeline_mode=pl.Buffered(1))],
        out_specs=pl.BlockSpec((tm, n), lambda i: (i, 0)),
        scratch_shapes=[pltpu.VMEM((tm, d), BF16)],
        compiler_params=_params(("parallel",), est),
        name="in_proj",
    )(x2, g, w)


def _attn_window(rows):
    nblk = rows // Q_ROWS
    kr = min(WIN_R, rows)
    n_dr = 2 * WIN_R - 1
    q_row0 = np.array([0, Q_ROWS, rows - Q_ROWS])
    k_row0 = np.array([0, 0, (nblk - K_BLOCKS) * Q_ROWS])
    qr = q_row0[:, None] + np.arange(Q_ROWS)[None, :]
    kro = k_row0[:, None] + np.arange(K_BLOCKS * Q_ROWS)[None, :]
    rstart = np.clip(qr - kr // 2, 0, rows - kr)
    row_ok = (kro[:, None, :] >= rstart[:, :, None]) & (kro[:, None, :] < rstart[:, :, None] + kr)
    dr = np.clip(kro[:, None, :] - qr[:, :, None] + WIN_R - 1, 0, n_dr - 1)
    return row_ok, dr


def _bias_kernel(rpb_ref, o_ref, *, row_ok, dr):
    lanes = 2 * GRID_W
    qc = lax.broadcasted_iota(jnp.int32, (GRID_W, lanes), 0)
    lane = lax.broadcasted_iota(jnp.int32, (GRID_W, lanes), 1)
    kc = lane & (GRID_W - 1)
    cstart = jnp.clip(qc - WIN_C // 2, 0, GRID_W - WIN_C)
    col_ok = (kc >= cstart) & (kc < cstart + WIN_C)
    first = lane < GRID_W
    neg = jnp.full((GRID_W, lanes), NEG_INF, F32)

    toeplitz = {}

    def block(r, second):
        if (r, second) not in toeplitz:
            row = jnp.broadcast_to(rpb_ref[r:r + 1, :], (GRID_W, lanes))
            t = pltpu.roll(row, lanes - (WIN_C - 1), 1, stride=1, stride_axis=0)
            toeplitz[(r, False)] = t
            toeplitz[(r, True)] = pltpu.roll(t, GRID_W, 1)
        return toeplitz[(r, second)]

    for t in range(3):
        for a in range(Q_ROWS):
            for pair in range(K_BLOCKS * Q_ROWS // 2):
                l0, l1 = 2 * pair, 2 * pair + 1
                left = block(int(dr[t, a, l0]), False) if row_ok[t, a, l0] else neg
                right = block(int(dr[t, a, l1]), True) if row_ok[t, a, l1] else neg
                tile = jnp.where(col_ok, jnp.where(first, left, right), neg)
                o_ref[t, a * GRID_W:(a + 1) * GRID_W, pair * lanes:(pair + 1) * lanes] = tile


def _attn_bias(rpb, rows):
    n_heads, n_dr, n_dc = rpb.shape
    row_ok, dr = _attn_window(rows)
    padded = jnp.zeros((n_heads, -(-n_dr // V7X_SUBLANES) * V7X_SUBLANES, 2 * GRID_W), F32)
    padded = padded.at[:, :n_dr, :n_dc].set(rpb.astype(F32))
    shape = (3, n_heads, Q_TOK, K_BLOCKS * Q_TOK)
    return pl.pallas_call(
        functools.partial(_bias_kernel, row_ok=row_ok, dr=dr),
        out_shape=jax.ShapeDtypeStruct(shape, F32),
        grid=(n_heads,),
        in_specs=[pl.BlockSpec((None,) + padded.shape[1:], lambda h: (h, 0, 0))],
        out_specs=pl.BlockSpec((3, None) + shape[2:], lambda h: (0, h, 0, 0)),
        compiler_params=_params(("parallel",), 2 * 3 * shape[2] * shape[3] * 4),
        name="attn_bias",
    )(padded)


def _attn_kernel(q_ref, k0_ref, k1_ref, k2_ref, v0_ref, v1_ref, v2_ref, bias_ref, o_ref, *, n_heads):
    k_refs = (k0_ref, k1_ref, k2_ref)
    v_refs = (v0_ref, v1_ref, v2_ref)
    scale = HEAD_DIM ** -0.5
    for h in range(n_heads):
        hs = slice(h * HEAD_DIM, (h + 1) * HEAD_DIM)
        q = (q_ref[:, hs].astype(F32) * scale).astype(BF16)
        s = [lax.dot_general(q, k_refs[i][:, hs], (((1,), (1,)), ((), ())),
                             preferred_element_type=F32)
             + bias_ref[h, :, i * Q_TOK:(i + 1) * Q_TOK] for i in range(K_BLOCKS)]
        lane_tiles = lambda a: [a[:, c:c + HEAD_DIM] for c in range(0, Q_TOK, HEAD_DIM)]
        m = functools.reduce(jnp.maximum, [t for si in s for t in lane_tiles(si)])
        m = m.max(axis=-1, keepdims=True)
        p = [jnp.exp(si - m) for si in s]
        l = functools.reduce(jnp.add, [t for pi in p for t in lane_tiles(pi)])
        l = l.sum(axis=-1, keepdims=True)
        acc = jnp.dot(p[0].astype(BF16), v_refs[0][:, hs], preferred_element_type=F32)
        for i in range(1, K_BLOCKS):
            acc = acc + jnp.dot(p[i].astype(BF16), v_refs[i][:, hs], preferred_element_type=F32)
        o_ref[:, hs] = (acc * (1.0 / l)).astype(o_ref.dtype)


def _attention(u, bias, *, batch, seq, d_attn):
    t = u.shape[0]
    n_heads = d_attn // HEAD_DIM
    nblk = seq // Q_TOK
    assert nblk >= K_BLOCKS

    def q_map(b, j):
        return (b * nblk + j, 0)

    def kv_map(i, col):
        def f(b, j):
            return (b * nblk + jnp.clip(j - 1, 0, nblk - K_BLOCKS) + i, col)
        return f

    def bias_map(b, j):
        return (jnp.where(j == 0, 0, jnp.where(j == nblk - 1, 2, 1)), 0, 0, 0)

    blk = pl.BlockSpec((Q_TOK, d_attn), q_map)
    in_specs = [blk]
    in_specs += [pl.BlockSpec((Q_TOK, d_attn), kv_map(i, 1)) for i in range(K_BLOCKS)]
    in_specs += [pl.BlockSpec((Q_TOK, d_attn), kv_map(i, 2)) for i in range(K_BLOCKS)]
    in_specs += [pl.BlockSpec((None, n_heads, Q_TOK, K_BLOCKS * Q_TOK), bias_map)]
    est = 2 * 8 * Q_TOK * d_attn * 2 + 2 * n_heads * Q_TOK * K_BLOCKS * Q_TOK * 4
    return pl.pallas_call(
        functools.partial(_attn_kernel, n_heads=n_heads),
        out_shape=jax.ShapeDtypeStruct((t, d_attn), BF16),
        grid=(batch, nblk),
        in_specs=in_specs,
        out_specs=blk,
        compiler_params=_params(("parallel", "arbitrary"), est),
        name="nattn",
    )(u, u, u, u, u, u, u, bias)


def _scan_pitch(n_chunks):
    groups = n_chunks // V7X_SUBLANES
    return (groups + 1 - groups % 2) * V7X_SUBLANES


def _sigmoid(x):
    return 0.5 * jnp.tanh(0.5 * x) + 0.5


def _rglru_kernel(xr_ref, yg_ref, cw_ref, cb_ref, wg_ref, bg_ref, lam_ref, o_ref,
                  xp_ref, a_ref, b_ref, h_ref, cr_ref, *, seq):
    n_chunks = seq // SCAN_CHUNK
    pitch = _scan_pitch(n_chunks)
    w = REC_BLOCK_W
    halo = V7X_BF16_ROWS
    win_rows = SCAN_CHUNK + 2 * halo

    zeros = jnp.zeros((halo, w), xp_ref.dtype)
    xp_ref[0:halo, :] = zeros
    xp_ref[seq + halo:seq + 2 * halo, :] = zeros

    def stage(c, carry):
        r = pl.multiple_of(c * SCAN_CHUNK, SCAN_CHUNK)
        xp_ref[pl.ds(r + halo, SCAN_CHUNK), :] = xr_ref[pl.ds(r, SCAN_CHUNK), :]
        return carry
    lax.fori_loop(0, n_chunks, stage, 0)

    z = -lam_ref[...]
    half_coef = (-0.5 * C_RG) * (jnp.maximum(z, 0.0) + jnp.log1p(jnp.exp(-jnp.abs(z))))

    left = CONV_W // 2

    def gates(c, carry):
        r = pl.multiple_of(c * SCAN_CHUNK, SCAN_CHUNK)
        win = xp_ref[pl.ds(r, win_rows), :].astype(F32)
        xc = cb_ref[...]
        for j in range(CONV_W):
            back = left - j
            tap = win if back == 0 else pltpu.roll(win, back % win_rows, 0)
            xc = xc + tap[halo:halo + SCAN_CHUNK] * cw_ref[j:j + 1, :]
        th = jnp.tanh(jnp.dot(xc.astype(BF16), wg_ref[...], preferred_element_type=F32) + bg_ref[...])
        xh = 0.5 * xc
        chunk_rows = pl.ds(c, SCAN_CHUNK, stride=pitch)
        for d in range(2):
            a = jnp.exp(th[:, d * w:(d + 1) * w] * half_coef[d:d + 1, :] + half_coef[d:d + 1, :])
            y = 1.0 - a * a
            root = jnp.where(y > 0.0, y * lax.rsqrt(y), 0.0)
            a_ref[d, chunk_rows, :] = a
            b_ref[d, chunk_rows, :] = root * (th[:, (2 + d) * w:(3 + d) * w] + 1.0) * xh
        return carry
    lax.fori_loop(0, n_chunks, gates, 0)

    def step_rows(d, t):
        pos = t if d == 0 else SCAN_CHUNK - 1 - t
        return pl.ds(pl.multiple_of(pos * pitch, V7X_SUBLANES), n_chunks)

    def pass1(t, carry):
        hs, ps = carry
        a = [a_ref[d, step_rows(d, t), :] for d in range(2)]
        return (tuple(a[d] * hs[d] + b_ref[d, step_rows(d, t), :] for d in range(2)),
                tuple(a[d] * ps[d] for d in range(2)))

    zero = jnp.zeros((n_chunks, w), F32)
    he, pe = lax.fori_loop(0, SCAN_CHUNK, pass1, ((zero, zero), (zero + 1.0, zero + 1.0)),
                           unroll=SCAN_UNROLL)

    for d in range(2):
        order = range(n_chunks) if d == 0 else range(n_chunks - 1, -1, -1)
        carry = jnp.zeros((1, w), F32)
        for j in order:
            cr_ref[d, j:j + 1, :] = carry
            carry = he[d][j:j + 1, :] + pe[d][j:j + 1, :] * carry

    def pass2(t, hs):
        nh = []
        for d in range(2):
            h = a_ref[d, step_rows(d, t), :] * hs[d] + b_ref[d, step_rows(d, t), :]
            h_ref[d, step_rows(d, t), :] = h
            nh.append(h)
        return tuple(nh)

    lax.fori_loop(0, SCAN_CHUNK, pass2, (cr_ref[0], cr_ref[1]), unroll=SCAN_UNROLL)

    def combine(c, carry):
        r = pl.multiple_of(c * SCAN_CHUNK, SCAN_CHUNK)
        chunk_rows = pl.ds(c, SCAN_CHUNK, stride=pitch)
        h = h_ref[0, chunk_rows, :] + h_ref[1, chunk_rows, :]
        y = yg_ref[pl.ds(r, SCAN_CHUNK), :].astype(F32)
        o_ref[pl.ds(r, SCAN_CHUNK), :] = (h * jax.nn.gelu(y)).astype(o_ref.dtype)
        return carry
    lax.fori_loop(0, n_chunks, combine, 0)


def _rglru(u3, conv_w, conv_b, wg, bg, lam, *, d_attn, d_rec):
    b, s, _ = u3.shape
    w = REC_BLOCK_W
    nb = d_rec // w
    assert s % (SCAN_CHUNK * V7X_SUBLANES) == 0
    n_chunks = s // SCAN_CHUNK
    x_col0 = 3 * d_attn // w
    y_col0 = x_col0 + nb
    scan_buf = pltpu.VMEM((2, SCAN_CHUNK * _scan_pitch(n_chunks), w), F32)
    est = (3 * 2 * s * w * 2 + (s + 2 * V7X_BF16_ROWS) * w * 2
           + 3 * 2 * SCAN_CHUNK * _scan_pitch(n_chunks) * w * 4)
    return pl.pallas_call(
        functools.partial(_rglru_kernel, seq=s),
        out_shape=jax.ShapeDtypeStruct((b, s, d_rec), BF16),
        grid=(b, nb),
        in_specs=[pl.BlockSpec((None, s, w), lambda i, n: (i, 0, x_col0 + n)),
                  pl.BlockSpec((None, s, w), lambda i, n: (i, 0, y_col0 + n)),
                  pl.BlockSpec((CONV_W, w), lambda i, n: (0, n)),
                  pl.BlockSpec((1, w), lambda i, n: (0, n)),
                  pl.BlockSpec((None, w, 4 * w), lambda i, n: (n, 0, 0)),
                  pl.BlockSpec((None, 1, 4 * w), lambda i, n: (n, 0, 0)),
                  pl.BlockSpec((2, w), lambda i, n: (0, n))],
        out_specs=pl.BlockSpec((None, s, w), lambda i, n: (i, 0, n)),
        scratch_shapes=[pltpu.VMEM((s + 2 * V7X_BF16_ROWS, w), BF16),
                        scan_buf, scan_buf, scan_buf, pltpu.VMEM((2, n_chunks, w), F32)],
        compiler_params=_params(("parallel", "arbitrary"), est),
        name="rglru",
    )(u3, u3, conv_w, conv_b, wg, bg, lam)


def _outproj_kernel(a_ref, r_ref, x_ref, ga_ref, gr_ref, w_ref, gp_ref, o_ref, cat_ref):
    da = a_ref.shape[1]
    cat_ref[:, :da] = _rms(a_ref[...].astype(F32), ga_ref[...]).astype(BF16)
    cat_ref[:, da:] = _rms(r_ref[...].astype(F32), gr_ref[...]).astype(BF16)
    mixed = jnp.dot(cat_ref[...], w_ref[...], preferred_element_type=F32)
    o_ref[...] = x_ref[...] + _rms(mixed, gp_ref[...])


def _out_proj(attn, rec, x2, g_attn, g_rec, w, g_post, *, tm):
    t, d = x2.shape
    da, dr = attn.shape[1], rec.shape[1]
    est = 2 * tm * (da + dr) * 2 + 4 * tm * d * 4 + (da + dr) * d * 2 + tm * (da + dr) * 2
    row = lambda i: (i, 0)
    fixed = lambda i: (0, 0)
    return pl.pallas_call(
        _outproj_kernel,
        out_shape=jax.ShapeDtypeStruct((t, d), F32),
        grid=(t // tm,),
        in_specs=[pl.BlockSpec((tm, da), row), pl.BlockSpec((tm, dr), row), pl.BlockSpec((tm, d), row),
                  pl.BlockSpec((1, da), fixed), pl.BlockSpec((1, dr), fixed),
                  pl.BlockSpec((da + dr, d), fixed, pipeline_mode=pl.Buffered(1)),
                  pl.BlockSpec((1, d), fixed)],
        out_specs=pl.BlockSpec((tm, d), row),
        scratch_shapes=[pltpu.VMEM((tm, da + dr), BF16)],
        compiler_params=_params(("parallel",), est),
        name="out_proj",
    )(attn, rec, x2, g_attn, g_rec, w, g_post)


def _ffn_kernel(h_ref, gpre_ref, wg_ref, wu_ref, wd_ref, gpost_ref, o_ref, fn_ref, acc_ref, *, row_chunk):
    f = pl.program_id(1)

    @pl.when(f == 0)
    def _():
        def body(c, carry):
            r = pl.multiple_of(c * row_chunk, row_chunk)
            fn_ref[pl.ds(r, row_chunk), :] = _rms(h_ref[pl.ds(r, row_chunk), :], gpre_ref[...]).astype(BF16)
            return carry
        lax.fori_loop(0, h_ref.shape[0] // row_chunk, body, 0)

    fn = fn_ref[...]
    gate = jnp.dot(fn, wg_ref[...], preferred_element_type=F32)
    up = jnp.dot(fn, wu_ref[...], preferred_element_type=F32)
    act = (jax.nn.silu(gate) * up).astype(BF16)
    @pl.when(f == 0)
    def _():
        acc_ref[...] = jnp.dot(act, wd_ref[...], preferred_element_type=F32)

    @pl.when(f > 0)
    def _():
        acc_ref[...] += jnp.dot(act, wd_ref[...], preferred_element_type=F32)

    @pl.when(f == pl.num_programs(1) - 1)
    def _():
        def body(c, carry):
            r = pl.multiple_of(c * row_chunk, row_chunk)
            o_ref[pl.ds(r, row_chunk), :] = _rms(acc_ref[pl.ds(r, row_chunk), :], gpost_ref[...]).astype(o_ref.dtype)
            return carry
        lax.fori_loop(0, h_ref.shape[0] // row_chunk, body, 0)


def _ffn(h1, g_pre, w_gate, w_up, w_down, g_post, *, tm, tf):
    t, d = h1.shape
    dff = w_gate.shape[1]
    est = 2 * tm * d * 4 + 3 * 2 * d * tf * 2 + 2 * tm * d * 2 + tm * d * 2 + tm * d * 4
    return pl.pallas_call(
        functools.partial(_ffn_kernel, row_chunk=min(128, tm)),
        out_shape=jax.ShapeDtypeStruct((t, d), BF16),
        grid=(t // tm, dff // tf),
        in_specs=[pl.BlockSpec((tm, d), lambda i, f: (i, 0)),
                  pl.BlockSpec((1, d), lambda i, f: (0, 0)),
                  pl.BlockSpec((d, tf), lambda i, f: (0, f)),
                  pl.BlockSpec((d, tf), lambda i, f: (0, f)),
                  pl.BlockSpec((tf, d), lambda i, f: (f, 0)),
                  pl.BlockSpec((1, d), lambda i, f: (0, 0))],
        out_specs=pl.BlockSpec((tm, d), lambda i, f: (i, 0)),
        scratch_shapes=[pltpu.VMEM((tm, d), BF16), pltpu.VMEM((tm, d), F32)],
        compiler_params=_params(("parallel", "arbitrary"), est),
        name="ffn",
    )(h1, g_pre, w_gate, w_up, w_down, g_post)


def _ple_kernel(h_ref, d_ref, p_ref, gpre_ref, wg_ref, wp_ref, gpost_ref, o_ref):
    h2 = h_ref[...] + d_ref[...].astype(F32)
    gate = _sigmoid(jnp.dot(_rms(h2, gpre_ref[...]).astype(BF16), wg_ref[...],
                            preferred_element_type=F32))
    ple = jnp.dot(p_ref[...].astype(BF16), wp_ref[...], preferred_element_type=F32)
    o_ref[...] = h2 + _rms(gate * ple, gpost_ref[...])


def _ple(h1, dff, p2, g_pre, w_gate, w_proj, g_post, *, tm):
    t, d = h1.shape
    dp = p2.shape[1]
    est = 4 * tm * d * 4 + 2 * tm * d * 2 + 2 * tm * dp * 4 + d * d * 2 + 2 * dp * d * 2 + 3 * tm * d * 4
    row = lambda i: (i, 0)
    fixed = lambda i: (0, 0)
    return pl.pallas_call(
        _ple_kernel,
        out_shape=jax.ShapeDtypeStruct((t, d), F32),
        grid=(t // tm,),
        in_specs=[pl.BlockSpec((tm, d), row), pl.BlockSpec((tm, d), row), pl.BlockSpec((tm, dp), row),
                  pl.BlockSpec((1, d), fixed),
                  pl.BlockSpec((d, d), fixed, pipeline_mode=pl.Buffered(1)),
                  pl.BlockSpec((dp, d), fixed),
                  pl.BlockSpec((1, d), fixed)],
        out_specs=pl.BlockSpec((tm, d), row),
        compiler_params=_params(("parallel",), est),
        name="ple",
    )(h1, dff, p2, g_pre, w_gate, w_proj, g_post)


def _layer(h, p_i, g_mix_pre, w_in, rpb, conv_w, conv_b, w_rg_a, b_rg_a, w_rg_i, b_rg_i, lam,
           g_attn_out, g_rec_out, w_out, g_mix_post, g_ffn_pre, w_ffn_gate, w_ffn_up, w_ffn_down,
           g_ffn_post, g_ple_pre, w_ple_gate, w_ple_proj, g_ple_post):
    batch, seq, d = h.shape
    t = batch * seq
    d_attn = g_attn_out.shape[0]
    d_rec = g_rec_out.shape[0]
    nb = d_rec // REC_BLOCK_W
    row = lambda v: v.reshape(1, -1).astype(F32)

    x2 = h.reshape(t, d)
    u = _in_proj(x2, row(g_mix_pre), w_in.astype(BF16), tm=min(512, t), tn=1024)

    attn = _attention(u, _attn_bias(rpb, seq // GRID_W), batch=batch, seq=seq, d_attn=d_attn)

    wg = (0.5 * jnp.concatenate([w_rg_a[0], w_rg_a[1], w_rg_i[0], w_rg_i[1]], axis=-1)).astype(BF16)
    bg = jnp.concatenate([b_rg_a.reshape(2, nb, 1, REC_BLOCK_W)[0], b_rg_a.reshape(2, nb, 1, REC_BLOCK_W)[1],
                          b_rg_i.reshape(2, nb, 1, REC_BLOCK_W)[0], b_rg_i.reshape(2, nb, 1, REC_BLOCK_W)[1]],
                         axis=-1).astype(F32) * 0.5
    rec = _rglru(u.reshape(batch, seq, -1), conv_w.astype(F32), row(conv_b), wg, bg, lam.astype(F32),
                 d_attn=d_attn, d_rec=d_rec)

    h1 = _out_proj(attn, rec.reshape(t, d_rec), x2, row(g_attn_out), row(g_rec_out),
                   w_out.astype(BF16), row(g_mix_post), tm=min(512, t))

    dff = _ffn(h1, row(g_ffn_pre), w_ffn_gate.astype(BF16), w_ffn_up.astype(BF16),
               w_ffn_down.astype(BF16), row(g_ffn_post), tm=min(1024, t), tf=512)

    out = _ple(h1, dff, p_i.reshape(t, -1), row(g_ple_pre), w_ple_gate.astype(BF16),
               w_ple_proj.astype(BF16), row(g_ple_post), tm=min(512, t))
    return out.reshape(batch, seq, d)


def kernel(x, p, g_mix_pre, w_in, rpb, conv_w, conv_b, w_rg_a, b_rg_a, w_rg_i, b_rg_i, lam, g_attn_out, g_rec_out, w_out, g_mix_post, g_ffn_pre, w_ffn_gate, w_ffn_up, w_ffn_down, g_ffn_post, g_ple_pre, w_ple_gate, w_ple_proj, g_ple_post):
    h = x
    for i in range(p.shape[0]):
        h = _layer(h, p[i], g_mix_pre[i], w_in[i], rpb[i], conv_w[i], conv_b[i], w_rg_a[i], b_rg_a[i],
                   w_rg_i[i], b_rg_i[i], lam[i], g_attn_out[i], g_rec_out[i], w_out[i], g_mix_post[i],
                   g_ffn_pre[i], w_ffn_gate[i], w_ffn_up[i], w_ffn_down[i], g_ffn_post[i], g_ple_pre[i],
                   w_ple_gate[i], w_ple_proj[i], g_ple_post[i])
    return h
```

```python
import functools

import numpy as np
import jax
import jax.numpy as jnp
from jax import lax
from jax.experimental import pallas as pl
from jax.experimental.pallas import tpu as pltpu

F32 = jnp.float32
BF16 = jnp.bfloat16

EPS = 1e-6
NEG_INF = -1e9
LOG2_E = 1.4426950408889634
GRID_W = 64
HEAD_DIM = 128
WIN_R = 8
WIN_C = 16
CONV_W = 4
C_RG = 8.0
REC_BLOCK_W = 128

V7X_SUBLANES = 8
V7X_BF16_ROWS = 16
V7X_VMEM_BYTES = 64 * 1024 * 1024
V7X_VMEM_CAP = V7X_VMEM_BYTES - 8 * 1024 * 1024

Q_ROWS = 4
Q_TOK = Q_ROWS * GRID_W
K_BLOCKS = 3

SCAN_CHUNK = 256
SCAN_UNROLL = 8


def _vmem_limit(estimate_bytes):
    return int(min(estimate_bytes * 5 // 4 + (4 << 20), V7X_VMEM_CAP))


def _params(semantics, estimate_bytes):
    return pltpu.CompilerParams(dimension_semantics=semantics,
                                vmem_limit_bytes=_vmem_limit(estimate_bytes))


def _rms(x, g):
    ms = jnp.mean(x * x, axis=-1, keepdims=True)
    return x * lax.rsqrt(ms + EPS) * g


def _inproj_kernel(x_ref, g_ref, w_ref, o_ref, hn_ref, *, tn):
    hn_ref[...] = _rms(x_ref[...], g_ref[...]).astype(BF16)
    for c in range(0, o_ref.shape[1], tn):
        o_ref[:, c:c + tn] = jnp.dot(hn_ref[...], w_ref[:, c:c + tn],
                                     preferred_element_type=F32).astype(o_ref.dtype)


def _in_proj(x2, g, w, *, tm, tn):
    t, d = x2.shape
    n = w.shape[1]
    est = 2 * tm * d * 4 + d * n * 2 + 2 * tm * n * 2 + tm * d * 2 + tm * d * 4
    return pl.pallas_call(
        functools.partial(_inproj_kernel, tn=tn),
        out_shape=jax.ShapeDtypeStruct((t, n), BF16),
        grid=(t // tm,),
        in_specs=[pl.BlockSpec((tm, d), lambda i: (i, 0)),
                  pl.BlockSpec((1, d), lambda i: (0, 0)),
                  pl.BlockSpec((d, n), lambda i: (0, 0), pipeline_mode=pl.Buffered(1))],
        out_specs=pl.BlockSpec((tm, n), lambda i: (i, 0)),
        scratch_shapes=[pltpu.VMEM((tm, d), BF16)],
        compiler_params=_params(("parallel",), est),
        name="in_proj",
    )(x2, g, w)


def _attn_window(rows):
    nblk = rows // Q_ROWS
    kr = min(WIN_R, rows)
    n_dr = 2 * WIN_R - 1
    q_row0 = np.array([0, Q_ROWS, rows - Q_ROWS])
    k_row0 = np.array([0, 0, (nblk - K_BLOCKS) * Q_ROWS])
    qr = q_row0[:, None] + np.arange(Q_ROWS)[None, :]
    kro = k_row0[:, None] + np.arange(K_BLOCKS * Q_ROWS)[None, :]
    rstart = np.clip(qr - kr // 2, 0, rows - kr)
    row_ok = (kro[:, None, :] >= rstart[:, :, None]) & (kro[:, None, :] < rstart[:, :, None] + kr)
    dr = np.clip(kro[:, None, :] - qr[:, :, None] + WIN_R - 1, 0, n_dr - 1)
    return row_ok, dr


def _bias_kernel(rpb_ref, o_ref, *, row_ok, dr):
    lanes = 2 * GRID_W
    qc = lax.broadcasted_iota(jnp.int32, (GRID_W, lanes), 0)
    lane = lax.broadcasted_iota(jnp.int32, (GRID_W, lanes), 1)
    kc = lane & (GRID_W - 1)
    cstart = jnp.clip(qc - WIN_C // 2, 0, GRID_W - WIN_C)
    col_ok = (kc >= cstart) & (kc < cstart + WIN_C)
    first = lane < GRID_W
    neg = jnp.full((GRID_W, lanes), NEG_INF, F32)

    toeplitz = {}

    def block(r, second):
        if (r, second) not in toeplitz:
            row = jnp.broadcast_to(rpb_ref[r:r + 1, :], (GRID_W, lanes))
            t = pltpu.roll(row, lanes - (WIN_C - 1), 1, stride=1, stride_axis=0)
            toeplitz[(r, False)] = t
            toeplitz[(r, True)] = pltpu.roll(t, GRID_W, 1)
        return toeplitz[(r, second)]

    for t in range(3):
        for a in range(Q_ROWS):
            for pair in range(K_BLOCKS * Q_ROWS // 2):
                l0, l1 = 2 * pair, 2 * pair + 1
                left = block(int(dr[t, a, l0]), False) if row_ok[t, a, l0] else neg
                right = block(int(dr[t, a, l1]), True) if row_ok[t, a, l1] else neg
                tile = jnp.where(col_ok, jnp.where(first, left, right), neg)
                o_ref[t, a * GRID_W:(a + 1) * GRID_W, pair * lanes:(pair + 1) * lanes] = tile


def _attn_bias(rpb, rows):
    n_heads, n_dr, n_dc = rpb.shape
    row_ok, dr = _attn_window(rows)
    padded = jnp.zeros((n_heads, -(-n_dr // V7X_SUBLANES) * V7X_SUBLANES, 2 * GRID_W), F32)
    padded = padded.at[:, :n_dr, :n_dc].set(rpb.astype(F32))
    shape = (3, n_heads, Q_TOK, K_BLOCKS * Q_TOK)
    return pl.pallas_call(
        functools.partial(_bias_kernel, row_ok=row_ok, dr=dr),
        out_shape=jax.ShapeDtypeStruct(shape, F32),
        grid=(n_heads,),
        in_specs=[pl.BlockSpec((None,) + padded.shape[1:], lambda h: (h, 0, 0))],
        out_specs=pl.BlockSpec((3, None) + shape[2:], lambda h: (0, h, 0, 0)),
        compiler_params=_params(("parallel",), 2 * 3 * shape[2] * shape[3] * 4),
        name="attn_bias",
    )(padded)


def _attn_heads(q_ref, k_refs, v_refs, bias_ref, o_ref, *, n_heads, need):
    scale = HEAD_DIM ** -0.5
    n_tiles = len(need[0])
    per_blk = n_tiles // K_BLOCKS
    zeros = jnp.zeros((GRID_W, HEAD_DIM), BF16)
    for h in range(n_heads):
        hs = slice(h * HEAD_DIM, (h + 1) * HEAD_DIM)
        q = (q_ref[:, hs].astype(F32) * scale).astype(BF16)
        s = [lax.dot_general(q, k_refs[i][:, hs], (((1,), (1,)), ((), ())),
                             preferred_element_type=F32) for i in range(K_BLOCKS)]
        p_rows, l_rows = [], []
        for a in range(Q_ROWS):
            rs = slice(a * GRID_W, (a + 1) * GRID_W)
            tiles = {t: s[t // per_blk][rs, (t % per_blk) * HEAD_DIM:(t % per_blk + 1) * HEAD_DIM]
                        + bias_ref[h, rs, t * HEAD_DIM:(t + 1) * HEAD_DIM]
                     for t in range(n_tiles) if need[a][t]}
            m = functools.reduce(jnp.maximum, tiles.values()).max(axis=-1, keepdims=True)
            p = {t: jnp.exp(v - m) for t, v in tiles.items()}
            l_rows.append(functools.reduce(jnp.add, p.values()).sum(axis=-1, keepdims=True))
            p_rows.append([p[t].astype(BF16) if t in p else zeros for t in range(n_tiles)])
        acc = None
        for i in range(K_BLOCKS):
            p_blk = jnp.concatenate(
                [jnp.concatenate(row[i * per_blk:(i + 1) * per_blk], axis=1) for row in p_rows], axis=0)
            part = jnp.dot(p_blk, v_refs[i][:, hs], preferred_element_type=F32)
            acc = part if acc is None else acc + part
        o_ref[:, hs] = (acc * (1.0 / jnp.concatenate(l_rows, axis=0))).astype(o_ref.dtype)


def _attn_kernel(q_ref, k0_ref, k1_ref, k2_ref, v0_ref, v1_ref, v2_ref, bias_ref, o_ref, *,
                 n_heads, need):
    j = pl.program_id(1)
    last = pl.num_programs(1) - 1
    block_type = jnp.where(j == 0, 0, jnp.where(j == last, 2, 1))
    for t in range(3):
        @pl.when(block_type == t)
        def _():
            _attn_heads(q_ref, (k0_ref, k1_ref, k2_ref), (v0_ref, v1_ref, v2_ref), bias_ref, o_ref,
                        n_heads=n_heads, need=need[t])


def _attention(u, bias, *, batch, seq, d_attn):
    t = u.shape[0]
    row_ok, _ = _attn_window(seq // GRID_W)
    pairs = row_ok.reshape(3, Q_ROWS, -1, HEAD_DIM // GRID_W)
    need = tuple(tuple(tuple(bool(x) for x in row) for row in typ) for typ in pairs.any(axis=-1))
    n_heads = d_attn // HEAD_DIM
    nblk = seq // Q_TOK
    assert nblk >= K_BLOCKS

    def q_map(b, j):
        return (b * nblk + j, 0)

    def kv_map(i, col):
        def f(b, j):
            return (b * nblk + jnp.clip(j - 1, 0, nblk - K_BLOCKS) + i, col)
        return f

    def bias_map(b, j):
        return (jnp.where(j == 0, 0, jnp.where(j == nblk - 1, 2, 1)), 0, 0, 0)

    blk = pl.BlockSpec((Q_TOK, d_attn), q_map)
    in_specs = [blk]
    in_specs += [pl.BlockSpec((Q_TOK, d_attn), kv_map(i, 1)) for i in range(K_BLOCKS)]
    in_specs += [pl.BlockSpec((Q_TOK, d_attn), kv_map(i, 2)) for i in range(K_BLOCKS)]
    in_specs += [pl.BlockSpec((None, n_heads, Q_TOK, K_BLOCKS * Q_TOK), bias_map)]
    est = 2 * 8 * Q_TOK * d_attn * 2 + 2 * n_heads * Q_TOK * K_BLOCKS * Q_TOK * 4
    return pl.pallas_call(
        functools.partial(_attn_kernel, n_heads=n_heads, need=need),
        out_shape=jax.ShapeDtypeStruct((t, d_attn), BF16),
        grid=(batch, nblk),
        in_specs=in_specs,
        out_specs=blk,
        compiler_params=_params(("parallel", "arbitrary"), est),
        name="nattn",
    )(u, u, u, u, u, u, u, bias)


def _scan_pitch(n_chunks):
    groups = n_chunks // V7X_SUBLANES
    return (groups + 1 - groups % 2) * V7X_SUBLANES


def _sigmoid(x):
    return 0.5 * jnp.tanh(0.5 * x) + 0.5


def _rglru_kernel(xr_ref, yg_ref, cw_ref, cb_ref, wg_ref, bg_ref, lam_ref, o_ref,
                  xp_ref, a_ref, b_ref, h_ref, cr_ref, *, seq):
    n_chunks = seq // SCAN_CHUNK
    pitch = _scan_pitch(n_chunks)
    w = REC_BLOCK_W
    halo = V7X_BF16_ROWS
    win_rows = SCAN_CHUNK + 2 * halo

    zeros = jnp.zeros((halo, w), xp_ref.dtype)
    xp_ref[0:halo, :] = zeros
    xp_ref[seq + halo:seq + 2 * halo, :] = zeros

    def stage(c, carry):
        r = pl.multiple_of(c * SCAN_CHUNK, SCAN_CHUNK)
        xp_ref[pl.ds(r + halo, SCAN_CHUNK), :] = xr_ref[pl.ds(r, SCAN_CHUNK), :]
        return carry
    lax.fori_loop(0, n_chunks, stage, 0)

    z = -lam_ref[...]
    half_coef = (-0.5 * C_RG * LOG2_E) * (jnp.maximum(z, 0.0) + jnp.log1p(jnp.exp(-jnp.abs(z))))

    left = CONV_W // 2

    def gates(c, carry):
        r = pl.multiple_of(c * SCAN_CHUNK, SCAN_CHUNK)
        win = xp_ref[pl.ds(r, win_rows), :].astype(F32)
        xc = cb_ref[...]
        for j in range(CONV_W):
            back = left - j
            tap = win if back == 0 else pltpu.roll(win, back % win_rows, 0)
            xc = xc + tap[halo:halo + SCAN_CHUNK] * cw_ref[j:j + 1, :]
        th = jnp.tanh(jnp.dot(xc.astype(BF16), wg_ref[...], preferred_element_type=F32) + bg_ref[...])
        xh = 0.5 * xc
        chunk_rows = pl.ds(c, SCAN_CHUNK, stride=pitch)
        for d in range(2):
            a = jnp.exp2(th[:, d * w:(d + 1) * w] * half_coef[d:d + 1, :] + half_coef[d:d + 1, :])
            y = 1.0 - a * a
            root = jnp.where(y > 0.0, y * lax.rsqrt(y), 0.0)
            a_ref[d, chunk_rows, :] = a
            b_ref[d, chunk_rows, :] = root * (th[:, (2 + d) * w:(3 + d) * w] + 1.0) * xh
        return carry
    lax.fori_loop(0, n_chunks, gates, 0, unroll=2)

    def step_rows(d, t):
        pos = t if d == 0 else SCAN_CHUNK - 1 - t
        return pl.ds(pl.multiple_of(pos * pitch, V7X_SUBLANES), n_chunks)

    def pass1(t, carry):
        hs, ps = carry
        a = [a_ref[d, step_rows(d, t), :] for d in range(2)]
        return (tuple(a[d] * hs[d] + b_ref[d, step_rows(d, t), :] for d in range(2)),
                tuple(a[d] * ps[d] for d in range(2)))

    zero = jnp.zeros((n_chunks, w), F32)
    he, pe = lax.fori_loop(0, SCAN_CHUNK, pass1, ((zero, zero), (zero + 1.0, zero + 1.0)),
                           unroll=SCAN_UNROLL)

    for d in range(2):
        order = range(n_chunks) if d == 0 else range(n_chunks - 1, -1, -1)
        carry = jnp.zeros((1, w), F32)
        for j in order:
            cr_ref[d, j:j + 1, :] = carry
            carry = he[d][j:j + 1, :] + pe[d][j:j + 1, :] * carry

    def pass2(t, hs):
        nh = []
        for d in range(2):
            h = a_ref[d, step_rows(d, t), :] * hs[d] + b_ref[d, step_rows(d, t), :]
            h_ref[d, step_rows(d, t), :] = h
            nh.append(h)
        return tuple(nh)

    lax.fori_loop(0, SCAN_CHUNK, pass2, (cr_ref[0], cr_ref[1]), unroll=SCAN_UNROLL)

    def combine(c, carry):
        r = pl.multiple_of(c * SCAN_CHUNK, SCAN_CHUNK)
        chunk_rows = pl.ds(c, SCAN_CHUNK, stride=pitch)
        h = h_ref[0, chunk_rows, :] + h_ref[1, chunk_rows, :]
        y = yg_ref[pl.ds(r, SCAN_CHUNK), :].astype(F32)
        o_ref[pl.ds(r, SCAN_CHUNK), :] = (h * jax.nn.gelu(y)).astype(o_ref.dtype)
        return carry
    lax.fori_loop(0, n_chunks, combine, 0)


def _rglru(u3, conv_w, conv_b, wg, bg, lam, *, d_attn, d_rec):
    b, s, _ = u3.shape
    w = REC_BLOCK_W
    nb = d_rec // w
    assert s % (SCAN_CHUNK * V7X_SUBLANES) == 0
    n_chunks = s // SCAN_CHUNK
    x_col0 = 3 * d_attn // w
    y_col0 = x_col0 + nb
    scan_buf = pltpu.VMEM((2, SCAN_CHUNK * _scan_pitch(n_chunks), w), F32)
    est = (3 * 2 * s * w * 2 + (s + 2 * V7X_BF16_ROWS) * w * 2
           + 3 * 2 * SCAN_CHUNK * _scan_pitch(n_chunks) * w * 4)
    return pl.pallas_call(
        functools.partial(_rglru_kernel, seq=s),
        out_shape=jax.ShapeDtypeStruct((b, s, d_rec), BF16),
        grid=(b, nb),
        in_specs=[pl.BlockSpec((None, s, w), lambda i, n: (i, 0, x_col0 + n)),
                  pl.BlockSpec((None, s, w), lambda i, n: (i, 0, y_col0 + n)),
                  pl.BlockSpec((CONV_W, w), lambda i, n: (0, n)),
                  pl.BlockSpec((1, w), lambda i, n: (0, n)),
                  pl.BlockSpec((None, w, 4 * w), lambda i, n: (n, 0, 0)),
                  pl.BlockSpec((None, 1, 4 * w), lambda i, n: (n, 0, 0)),
                  pl.BlockSpec((2, w), lambda i, n: (0, n))],
        out_specs=pl.BlockSpec((None, s, w), lambda i, n: (i, 0, n)),
        scratch_shapes=[pltpu.VMEM((s + 2 * V7X_BF16_ROWS, w), BF16),
                        scan_buf, scan_buf, scan_buf, pltpu.VMEM((2, n_chunks, w), F32)],
        compiler_params=_params(("parallel", "arbitrary"), est),
        name="rglru",
    )(u3, u3, conv_w, conv_b, wg, bg, lam)


def _outproj_kernel(a_ref, r_ref, x_ref, ga_ref, gr_ref, w_ref, gp_ref, o_ref, cat_ref):
    da = a_ref.shape[1]
    cat_ref[:, :da] = _rms(a_ref[...].astype(F32), ga_ref[...]).astype(BF16)
    cat_ref[:, da:] = _rms(r_ref[...].astype(F32), gr_ref[...]).astype(BF16)
    mixed = jnp.dot(cat_ref[...], w_ref[...], preferred_element_type=F32)
    o_ref[...] = x_ref[...] + _rms(mixed, gp_ref[...])


def _out_proj(attn, rec, x2, g_attn, g_rec, w, g_post, *, tm):
    t, d = x2.shape
    da, dr = attn.shape[1], rec.shape[1]
    est = 2 * tm * (da + dr) * 2 + 4 * tm * d * 4 + (da + dr) * d * 2 + tm * (da + dr) * 2
    row = lambda i: (i, 0)
    fixed = lambda i: (0, 0)
    return pl.pallas_call(
        _outproj_kernel,
        out_shape=jax.ShapeDtypeStruct((t, d), F32),
        grid=(t // tm,),
        in_specs=[pl.BlockSpec((tm, da), row), pl.BlockSpec((tm, dr), row), pl.BlockSpec((tm, d), row),
                  pl.BlockSpec((1, da), fixed), pl.BlockSpec((1, dr), fixed),
                  pl.BlockSpec((da + dr, d), fixed, pipeline_mode=pl.Buffered(1)),
                  pl.BlockSpec((1, d), fixed)],
        out_specs=pl.BlockSpec((tm, d), row),
        scratch_shapes=[pltpu.VMEM((tm, da + dr), BF16)],
        compiler_params=_params(("parallel",), est),
        name="out_proj",
    )(attn, rec, x2, g_attn, g_rec, w, g_post)


def _ffn_kernel(h_ref, gpre_ref, wg_ref, wu_ref, wd_ref, gpost_ref, o_ref, fn_ref, acc_ref, *, row_chunk):
    f = pl.program_id(1)

    @pl.when(f == 0)
    def _():
        def body(c, carry):
            r = pl.multiple_of(c * row_chunk, row_chunk)
            fn_ref[pl.ds(r, row_chunk), :] = _rms(h_ref[pl.ds(r, row_chunk), :], gpre_ref[...]).astype(BF16)
            return carry
        lax.fori_loop(0, h_ref.shape[0] // row_chunk, body, 0)

    fn = fn_ref[...]
    gate = jnp.dot(fn, wg_ref[...], preferred_element_type=F32)
    up = jnp.dot(fn, wu_ref[...], preferred_element_type=F32)
    act = (jax.nn.silu(gate) * up).astype(BF16)
    @pl.when(f == 0)
    def _():
        acc_ref[...] = jnp.dot(act, wd_ref[...], preferred_element_type=F32)

    @pl.when(f > 0)
    def _():
        acc_ref[...] += jnp.dot(act, wd_ref[...], preferred_element_type=F32)

    @pl.when(f == pl.num_programs(1) - 1)
    def _():
        def body(c, carry):
            r = pl.multiple_of(c * row_chunk, row_chunk)
            o_ref[pl.ds(r, row_chunk), :] = _rms(acc_ref[pl.ds(r, row_chunk), :], gpost_ref[...]).astype(o_ref.dtype)
            return carry
        lax.fori_loop(0, h_ref.shape[0] // row_chunk, body, 0)


def _ffn(h1, g_pre, w_gate, w_up, w_down, g_post, *, tm, tf):
    t, d = h1.shape
    dff = w_gate.shape[1]
    est = 2 * tm * d * 4 + 3 * 2 * d * tf * 2 + 2 * tm * d * 2 + tm * d * 2 + tm * d * 4
    return pl.pallas_call(
        functools.partial(_ffn_kernel, row_chunk=min(128, tm)),
        out_shape=jax.ShapeDtypeStruct((t, d), BF16),
        grid=(t // tm, dff // tf),
        in_specs=[pl.BlockSpec((tm, d), lambda i, f: (i, 0)),
                  pl.BlockSpec((1, d), lambda i, f: (0, 0)),
                  pl.BlockSpec((d, tf), lambda i, f: (0, f)),
                  pl.BlockSpec((d, tf), lambda i, f: (0, f)),
                  pl.BlockSpec((tf, d), lambda i, f: (f, 0)),
                  pl.BlockSpec((1, d), lambda i, f: (0, 0))],
        out_specs=pl.BlockSpec((tm, d), lambda i, f: (i, 0)),
        scratch_shapes=[pltpu.VMEM((tm, d), BF16), pltpu.VMEM((tm, d), F32)],
        compiler_params=_params(("parallel", "arbitrary"), est),
        name="ffn",
    )(h1, g_pre, w_gate, w_up, w_down, g_post)


def _ple_kernel(h_ref, d_ref, p_ref, gpre_ref, wg_ref, wp_ref, gpost_ref, o_ref):
    h2 = h_ref[...] + d_ref[...].astype(F32)
    gate = _sigmoid(jnp.dot(_rms(h2, gpre_ref[...]).astype(BF16), wg_ref[...],
                            preferred_element_type=F32))
    ple = jnp.dot(p_ref[...].astype(BF16), wp_ref[...], preferred_element_type=F32)
    o_ref[...] = h2 + _rms(gate * ple, gpost_ref[...])


def _ple(h1, dff, p2, g_pre, w_gate, w_proj, g_post, *, tm):
    t, d = h1.shape
    dp = p2.shape[1]
    est = 4 * tm * d * 4 + 2 * tm * d * 2 + 2 * tm * dp * 4 + d * d * 2 + 2 * dp * d * 2 + 3 * tm * d * 4
    row = lambda i: (i, 0)
    fixed = lambda i: (0, 0)
    return pl.pallas_call(
        _ple_kernel,
        out_shape=jax.ShapeDtypeStruct((t, d), F32),
        grid=(t // tm,),
        in_specs=[pl.BlockSpec((tm, d), row), pl.BlockSpec((tm, d), row), pl.BlockSpec((tm, dp), row),
                  pl.BlockSpec((1, d), fixed),
                  pl.BlockSpec((d, d), fixed, pipeline_mode=pl.Buffered(1)),
                  pl.BlockSpec((dp, d), fixed),
                  pl.BlockSpec((1, d), fixed)],
        out_specs=pl.BlockSpec((tm, d), row),
        compiler_params=_params(("parallel",), est),
        name="ple",
    )(h1, dff, p2, g_pre, w_gate, w_proj, g_post)


def _layer(h, p_i, g_mix_pre, w_in, rpb, conv_w, conv_b, w_rg_a, b_rg_a, w_rg_i, b_rg_i, lam,
           g_attn_out, g_rec_out, w_out, g_mix_post, g_ffn_pre, w_ffn_gate, w_ffn_up, w_ffn_down,
           g_ffn_post, g_ple_pre, w_ple_gate, w_ple_proj, g_ple_post):
    batch, seq, d = h.shape
    t = batch * seq
    d_attn = g_attn_out.shape[0]
    d_rec = g_rec_out.shape[0]
    nb = d_rec // REC_BLOCK_W
    row = lambda v: v.reshape(1, -1).astype(F32)

    x2 = h.reshape(t, d)
    u = _in_proj(x2, row(g_mix_pre), w_in.astype(BF16), tm=min(512, t), tn=1024)

    attn = _attention(u, _attn_bias(rpb, seq // GRID_W), batch=batch, seq=seq, d_attn=d_attn)

    wg = (0.5 * jnp.concatenate([w_rg_a[0], w_rg_a[1], w_rg_i[0], w_rg_i[1]], axis=-1)).astype(BF16)
    bg = jnp.concatenate([b_rg_a.reshape(2, nb, 1, REC_BLOCK_W)[0], b_rg_a.reshape(2, nb, 1, REC_BLOCK_W)[1],
                          b_rg_i.reshape(2, nb, 1, REC_BLOCK_W)[0], b_rg_i.reshape(2, nb, 1, REC_BLOCK_W)[1]],
                         axis=-1).astype(F32) * 0.5
    rec = _rglru(u.reshape(batch, seq, -1), conv_w.astype(F32), row(conv_b), wg, bg, lam.astype(F32),
                 d_attn=d_attn, d_rec=d_rec)

    h1 = _out_proj(attn, rec.reshape(t, d_rec), x2, row(g_attn_out), row(g_rec_out),
                   w_out.astype(BF16), row(g_mix_post), tm=min(512, t))

    dff = _ffn(h1, row(g_ffn_pre), w_ffn_gate.astype(BF16), w_ffn_up.astype(BF16),
               w_ffn_down.astype(BF16), row(g_ffn_post), tm=min(1024, t), tf=512)

    out = _ple(h1, dff, p_i.reshape(t, -1), row(g_ple_pre), w_ple_gate.astype(BF16),
               w_ple_proj.astype(BF16), row(g_ple_post), tm=min(512, t))
    return out.reshape(batch, seq, d)


def kernel(x, p, g_mix_pre, w_in, rpb, conv_w, conv_b, w_rg_a, b_rg_a, w_rg_i, b_rg_i, lam, g_attn_out, g_rec_out, w_out, g_mix_post, g_ffn_pre, w_ffn_gate, w_ffn_up, w_ffn_down, g_ffn_post, g_ple_pre, w_ple_gate, w_ple_proj, g_ple_post):
    h = x
    for i in range(p.shape[0]):
        h = _layer(h, p[i], g_mix_pre[i], w_in[i], rpb[i], conv_w[i], conv_b[i], w_rg_a[i], b_rg_a[i],
                   w_rg_i[i], b_rg_i[i], lam[i], g_attn_out[i], g_rec_out[i], w_out[i], g_mix_post[i],
                   g_ffn_pre[i], w_ffn_gate[i], w_ffn_up[i], w_ffn_down[i], g_ffn_post[i], g_ple_pre[i],
                   w_ple_gate[i], w_ple_proj[i], g_ple_post[i])
    return h
```

```python
import functools

import numpy as np
import jax
import jax.numpy as jnp
from jax import lax
from jax.experimental import pallas as pl
from jax.experimental.pallas import tpu as pltpu

F32 = jnp.float32
BF16 = jnp.bfloat16

EPS = 1e-6
NEG_INF = -1e9
LOG2_E = 1.4426950408889634
GRID_W = 64
HEAD_DIM = 128
WIN_R = 8
WIN_C = 16
CONV_W = 4
C_RG = 8.0
REC_BLOCK_W = 128

V7X_SUBLANES = 8
V7X_BF16_ROWS = 16
V7X_VMEM_BYTES = 64 * 1024 * 1024
V7X_VMEM_CAP = V7X_VMEM_BYTES - 8 * 1024 * 1024

Q_ROWS = 4
Q_TOK = Q_ROWS * GRID_W
K_BLOCKS = 3

SCAN_CHUNK = 256
GATE_UNROLL = 4
SCAN_UNROLL = 8


def _vmem_limit(estimate_bytes):
    return int(min(estimate_bytes * 5 // 4 + (4 << 20), V7X_VMEM_CAP))


def _params(semantics, estimate_bytes):
    return pltpu.CompilerParams(dimension_semantics=semantics,
                                vmem_limit_bytes=_vmem_limit(estimate_bytes))


def _rms(x, g):
    ms = jnp.mean(x * x, axis=-1, keepdims=True)
    return x * lax.rsqrt(ms + EPS) * g


def _inproj_kernel(*refs, tn, n_riders):
    x_ref, g_ref, w_ref = refs[:3]
    rider_src = refs[3:3 + n_riders]
    o_ref = refs[3 + n_riders]
    rider_dst = refs[4 + n_riders:4 + 2 * n_riders]
    hn_ref = refs[4 + 2 * n_riders]
    for src, dst in zip(rider_src, rider_dst):
        dst[...] = src[...].astype(dst.dtype)
    hn_ref[...] = _rms(x_ref[...], g_ref[...]).astype(BF16)
    for c in range(0, o_ref.shape[1], tn):
        o_ref[:, c:c + tn] = jnp.dot(hn_ref[...], w_ref[:, c:c + tn],
                                     preferred_element_type=F32).astype(o_ref.dtype)


def _rider_block(rows, steps):
    share = 1
    while (rows * share) % (steps * V7X_BF16_ROWS) != 0:
        share *= 2
    return rows * share // steps, share


def _in_proj(x2, g, w, riders, *, tm, tn):
    t, d = x2.shape
    n = w.shape[1]
    steps = t // tm
    est = 2 * tm * d * 4 + d * n * 2 + 2 * tm * n * 2 + tm * d * 2 + tm * d * 4
    rider_specs = []
    for r in riders:
        rows, share = _rider_block(r.shape[0], steps)
        rider_specs.append(pl.BlockSpec((rows, r.shape[1]), lambda i, share=share: (i // share, 0)))
        est += 2 * rows * r.shape[1] * (4 + 2)
    out = pl.pallas_call(
        functools.partial(_inproj_kernel, tn=tn, n_riders=len(riders)),
        out_shape=[jax.ShapeDtypeStruct((t, n), BF16)]
                  + [jax.ShapeDtypeStruct(r.shape, BF16) for r in riders],
        grid=(steps,),
        in_specs=[pl.BlockSpec((tm, d), lambda i: (i, 0)),
                  pl.BlockSpec((1, d), lambda i: (0, 0)),
                  pl.BlockSpec((d, n), lambda i: (0, 0), pipeline_mode=pl.Buffered(1))] + rider_specs,
        out_specs=[pl.BlockSpec((tm, n), lambda i: (i, 0))] + rider_specs,
        scratch_shapes=[pltpu.VMEM((tm, d), BF16)],
        compiler_params=_params(("arbitrary",), est),
        name="in_proj",
    )(x2, g, w, *riders)
    return out[0], out[1:]


def _attn_window(rows):
    nblk = rows // Q_ROWS
    kr = min(WIN_R, rows)
    n_dr = 2 * WIN_R - 1
    q_row0 = np.array([0, Q_ROWS, rows - Q_ROWS])
    k_row0 = np.array([0, 0, (nblk - K_BLOCKS) * Q_ROWS])
    qr = q_row0[:, None] + np.arange(Q_ROWS)[None, :]
    kro = k_row0[:, None] + np.arange(K_BLOCKS * Q_ROWS)[None, :]
    rstart = np.clip(qr - kr // 2, 0, rows - kr)
    row_ok = (kro[:, None, :] >= rstart[:, :, None]) & (kro[:, None, :] < rstart[:, :, None] + kr)
    dr = np.clip(kro[:, None, :] - qr[:, :, None] + WIN_R - 1, 0, n_dr - 1)
    return row_ok, dr


def _bias_kernel(rpb_ref, o_ref, *, row_ok, dr):
    lanes = 2 * GRID_W
    qc = lax.broadcasted_iota(jnp.int32, (GRID_W, lanes), 0)
    lane = lax.broadcasted_iota(jnp.int32, (GRID_W, lanes), 1)
    kc = lane & (GRID_W - 1)
    cstart = jnp.clip(qc - WIN_C // 2, 0, GRID_W - WIN_C)
    col_ok = (kc >= cstart) & (kc < cstart + WIN_C)
    first = lane < GRID_W
    neg = jnp.full((GRID_W, lanes), NEG_INF, F32)

    toeplitz = {}

    def block(r, second):
        if (r, second) not in toeplitz:
            row = jnp.broadcast_to(rpb_ref[r:r + 1, :], (GRID_W, lanes))
            t = pltpu.roll(row, lanes - (WIN_C - 1), 1, stride=1, stride_axis=0)
            toeplitz[(r, False)] = t
            toeplitz[(r, True)] = pltpu.roll(t, GRID_W, 1)
        return toeplitz[(r, second)]

    for t in range(3):
        for a in range(Q_ROWS):
            for pair in range(K_BLOCKS * Q_ROWS // 2):
                l0, l1 = 2 * pair, 2 * pair + 1
                left = block(int(dr[t, a, l0]), False) if row_ok[t, a, l0] else neg
                right = block(int(dr[t, a, l1]), True) if row_ok[t, a, l1] else neg
                tile = jnp.where(col_ok, jnp.where(first, left, right), neg)
                o_ref[t, a * GRID_W:(a + 1) * GRID_W, pair * lanes:(pair + 1) * lanes] = tile


def _attn_bias(rpb, rows):
    n_heads, n_dr, n_dc = rpb.shape
    row_ok, dr = _attn_window(rows)
    padded = jnp.zeros((n_heads, -(-n_dr // V7X_SUBLANES) * V7X_SUBLANES, 2 * GRID_W), F32)
    padded = padded.at[:, :n_dr, :n_dc].set(rpb.astype(F32))
    shape = (3, n_heads, Q_TOK, K_BLOCKS * Q_TOK)
    return pl.pallas_call(
        functools.partial(_bias_kernel, row_ok=row_ok, dr=dr),
        out_shape=jax.ShapeDtypeStruct(shape, F32),
        grid=(n_heads,),
        in_specs=[pl.BlockSpec((None,) + padded.shape[1:], lambda h: (h, 0, 0))],
        out_specs=pl.BlockSpec((3, None) + shape[2:], lambda h: (0, h, 0, 0)),
        compiler_params=_params(("parallel",), 2 * 3 * shape[2] * shape[3] * 4),
        name="attn_bias",
    )(padded)


def _attn_heads(q_ref, k_refs, v_refs, bias_ref, o_ref, *, n_heads, need):
    scale = HEAD_DIM ** -0.5
    n_tiles = len(need[0])
    per_blk = n_tiles // K_BLOCKS
    zeros = jnp.zeros((GRID_W, HEAD_DIM), BF16)
    for h in range(n_heads):
        hs = slice(h * HEAD_DIM, (h + 1) * HEAD_DIM)
        q = (q_ref[:, hs].astype(F32) * scale).astype(BF16)
        s = [lax.dot_general(q, k_refs[i][:, hs], (((1,), (1,)), ((), ())),
                             preferred_element_type=F32) for i in range(K_BLOCKS)]
        p_rows, l_rows = [], []
        for a in range(Q_ROWS):
            rs = slice(a * GRID_W, (a + 1) * GRID_W)
            tiles = {t: s[t // per_blk][rs, (t % per_blk) * HEAD_DIM:(t % per_blk + 1) * HEAD_DIM]
                        + bias_ref[h, rs, t * HEAD_DIM:(t + 1) * HEAD_DIM]
                     for t in range(n_tiles) if need[a][t]}
            m = functools.reduce(jnp.maximum, tiles.values()).max(axis=-1, keepdims=True)
            p = {t: jnp.exp(v - m) for t, v in tiles.items()}
            l_rows.append(functools.reduce(jnp.add, p.values()).sum(axis=-1, keepdims=True))
            p_rows.append([p[t].astype(BF16) if t in p else zeros for t in range(n_tiles)])
        acc = None
        for i in range(K_BLOCKS):
            p_blk = jnp.concatenate(
                [jnp.concatenate(row[i * per_blk:(i + 1) * per_blk], axis=1) for row in p_rows], axis=0)
            part = jnp.dot(p_blk, v_refs[i][:, hs], preferred_element_type=F32)
            acc = part if acc is None else acc + part
        o_ref[:, hs] = (acc * (1.0 / jnp.concatenate(l_rows, axis=0))).astype(o_ref.dtype)


def _attn_kernel(q_ref, k0_ref, k1_ref, k2_ref, v0_ref, v1_ref, v2_ref, bias_ref, o_ref, *,
                 n_heads, need):
    j = pl.program_id(1)
    last = pl.num_programs(1) - 1
    block_type = jnp.where(j == 0, 0, jnp.where(j == last, 2, 1))
    for t in range(3):
        @pl.when(block_type == t)
        def _():
            _attn_heads(q_ref, (k0_ref, k1_ref, k2_ref), (v0_ref, v1_ref, v2_ref), bias_ref, o_ref,
                        n_heads=n_heads, need=need[t])


def _attention(u, bias, *, batch, seq, d_attn):
    t = u.shape[0]
    row_ok, _ = _attn_window(seq // GRID_W)
    pairs = row_ok.reshape(3, Q_ROWS, -1, HEAD_DIM // GRID_W)
    need = tuple(tuple(tuple(bool(x) for x in row) for row in typ) for typ in pairs.any(axis=-1))
    n_heads = d_attn // HEAD_DIM
    nblk = seq // Q_TOK
    assert nblk >= K_BLOCKS

    def q_map(b, j):
        return (b * nblk + j, 0)

    def kv_map(i, col):
        def f(b, j):
            return (b * nblk + jnp.clip(j - 1, 0, nblk - K_BLOCKS) + i, col)
        return f

    def bias_map(b, j):
        return (jnp.where(j == 0, 0, jnp.where(j == nblk - 1, 2, 1)), 0, 0, 0)

    blk = pl.BlockSpec((Q_TOK, d_attn), q_map)
    in_specs = [blk]
    in_specs += [pl.BlockSpec((Q_TOK, d_attn), kv_map(i, 1)) for i in range(K_BLOCKS)]
    in_specs += [pl.BlockSpec((Q_TOK, d_attn), kv_map(i, 2)) for i in range(K_BLOCKS)]
    in_specs += [pl.BlockSpec((None, n_heads, Q_TOK, K_BLOCKS * Q_TOK), bias_map)]
    est = 2 * 8 * Q_TOK * d_attn * 2 + 2 * n_heads * Q_TOK * K_BLOCKS * Q_TOK * 4
    return pl.pallas_call(
        functools.partial(_attn_kernel, n_heads=n_heads, need=need),
        out_shape=jax.ShapeDtypeStruct((t, d_attn), BF16),
        grid=(batch, nblk),
        in_specs=in_specs,
        out_specs=blk,
        compiler_params=_params(("parallel", "arbitrary"), est),
        name="nattn",
    )(u, u, u, u, u, u, u, bias)


def _scan_pitch(n_chunks):
    groups = n_chunks // V7X_SUBLANES
    return (groups + 1 - groups % 2) * V7X_SUBLANES


def _sigmoid(x):
    return 0.5 * jnp.tanh(0.5 * x) + 0.5


def _rglru_kernel(xr_ref, yg_ref, cw_ref, cb_ref, wg_ref, bg_ref, lam_ref, o_ref,
                  xp_ref, a_ref, b_ref, h_ref, cr_ref, *, seq):
    n_chunks = seq // SCAN_CHUNK
    pitch = _scan_pitch(n_chunks)
    w = REC_BLOCK_W
    halo = V7X_BF16_ROWS

    zeros = jnp.zeros((halo, w), xp_ref.dtype)
    xp_ref[0:halo, :] = zeros
    xp_ref[seq + halo:seq + 2 * halo, :] = zeros

    def stage(c, carry):
        r = pl.multiple_of(c * SCAN_CHUNK, SCAN_CHUNK)
        xp_ref[pl.ds(r + halo, SCAN_CHUNK), :] = xr_ref[pl.ds(r, SCAN_CHUNK), :].astype(F32)
        return carry
    lax.fori_loop(0, n_chunks, stage, 0)

    z = -lam_ref[...]
    half_coef = (-0.5 * C_RG * LOG2_E) * (jnp.maximum(z, 0.0) + jnp.log1p(jnp.exp(-jnp.abs(z))))

    left = CONV_W // 2

    def gates(c, carry):
        r = pl.multiple_of(c * SCAN_CHUNK, SCAN_CHUNK)
        xc = cb_ref[...]
        for j in range(CONV_W):
            xc = xc + xp_ref[pl.ds(r + (halo + j - left), SCAN_CHUNK), :] * cw_ref[j:j + 1, :]
        th = jnp.tanh(jnp.dot(xc.astype(BF16), wg_ref[...], preferred_element_type=F32) + bg_ref[...])
        xh = 0.5 * xc
        chunk_rows = pl.ds(c, SCAN_CHUNK, stride=pitch)
        for d in range(2):
            a = jnp.exp2(th[:, d * w:(d + 1) * w] * half_coef[d:d + 1, :] + half_coef[d:d + 1, :])
            y = 1.0 - a * a
            root = jnp.where(y > 0.0, y * lax.rsqrt(y), 0.0)
            a_ref[d, chunk_rows, :] = a
            b_ref[d, chunk_rows, :] = root * (th[:, (2 + d) * w:(3 + d) * w] + 1.0) * xh
        return carry
    lax.fori_loop(0, n_chunks, gates, 0, unroll=GATE_UNROLL)

    def step_rows(d, t):
        pos = t if d == 0 else SCAN_CHUNK - 1 - t
        return pl.ds(pl.multiple_of(pos * pitch, V7X_SUBLANES), n_chunks)

    def pass1(t, carry):
        hs, ps = carry
        a = [a_ref[d, step_rows(d, t), :] for d in range(2)]
        return (tuple(a[d] * hs[d] + b_ref[d, step_rows(d, t), :] for d in range(2)),
                tuple(a[d] * ps[d] for d in range(2)))

    zero = jnp.zeros((n_chunks, w), F32)
    he, pe = lax.fori_loop(0, SCAN_CHUNK, pass1, ((zero, zero), (zero + 1.0, zero + 1.0)),
                           unroll=SCAN_UNROLL)

    for d in range(2):
        order = range(n_chunks) if d == 0 else range(n_chunks - 1, -1, -1)
        carry = jnp.zeros((1, w), F32)
        for j in order:
            cr_ref[d, j:j + 1, :] = carry
            carry = he[d][j:j + 1, :] + pe[d][j:j + 1, :] * carry

    def pass2(t, hs):
        nh = []
        for d in range(2):
            h = a_ref[d, step_rows(d, t), :] * hs[d] + b_ref[d, step_rows(d, t), :]
            h_ref[d, step_rows(d, t), :] = h
            nh.append(h)
        return tuple(nh)

    lax.fori_loop(0, SCAN_CHUNK, pass2, (cr_ref[0], cr_ref[1]), unroll=SCAN_UNROLL)

    def combine(c, carry):
        r = pl.multiple_of(c * SCAN_CHUNK, SCAN_CHUNK)
        chunk_rows = pl.ds(c, SCAN_CHUNK, stride=pitch)
        h = h_ref[0, chunk_rows, :] + h_ref[1, chunk_rows, :]
        y = yg_ref[pl.ds(r, SCAN_CHUNK), :].astype(F32)
        o_ref[pl.ds(r, SCAN_CHUNK), :] = (h * jax.nn.gelu(y)).astype(o_ref.dtype)
        return carry
    lax.fori_loop(0, n_chunks, combine, 0)


def _rglru(u3, conv_w, conv_b, wg, bg, lam, *, d_attn, d_rec):
    b, s, _ = u3.shape
    w = REC_BLOCK_W
    nb = d_rec // w
    assert s % (SCAN_CHUNK * V7X_SUBLANES) == 0
    n_chunks = s // SCAN_CHUNK
    x_col0 = 3 * d_attn // w
    y_col0 = x_col0 + nb
    scan_buf = pltpu.VMEM((2, SCAN_CHUNK * _scan_pitch(n_chunks), w), F32)
    est = (3 * 2 * s * w * 2 + (s + 2 * V7X_BF16_ROWS) * w * 4
           + 3 * 2 * SCAN_CHUNK * _scan_pitch(n_chunks) * w * 4)
    return pl.pallas_call(
        functools.partial(_rglru_kernel, seq=s),
        out_shape=jax.ShapeDtypeStruct((b, s, d_rec), BF16),
        grid=(b, nb),
        in_specs=[pl.BlockSpec((None, s, w), lambda i, n: (i, 0, x_col0 + n)),
                  pl.BlockSpec((None, s, w), lambda i, n: (i, 0, y_col0 + n)),
                  pl.BlockSpec((CONV_W, w), lambda i, n: (0, n)),
                  pl.BlockSpec((1, w), lambda i, n: (0, n)),
                  pl.BlockSpec((None, w, 4 * w), lambda i, n: (n, 0, 0)),
                  pl.BlockSpec((None, 1, 4 * w), lambda i, n: (n, 0, 0)),
                  pl.BlockSpec((2, w), lambda i, n: (0, n))],
        out_specs=pl.BlockSpec((None, s, w), lambda i, n: (i, 0, n)),
        scratch_shapes=[pltpu.VMEM((s + 2 * V7X_BF16_ROWS, w), F32),
                        scan_buf, scan_buf, scan_buf, pltpu.VMEM((2, n_chunks, w), F32)],
        compiler_params=_params(("parallel", "arbitrary"), est),
        name="rglru",
    )(u3, u3, conv_w, conv_b, wg, bg, lam)


def _outproj_kernel(a_ref, r_ref, x_ref, ga_ref, gr_ref, w_ref, gp_ref, o_ref, cat_ref):
    da = a_ref.shape[1]
    cat_ref[:, :da] = _rms(a_ref[...].astype(F32), ga_ref[...]).astype(BF16)
    cat_ref[:, da:] = _rms(r_ref[...].astype(F32), gr_ref[...]).astype(BF16)
    mixed = jnp.dot(cat_ref[...], w_ref[...], preferred_element_type=F32)
    o_ref[...] = x_ref[...] + _rms(mixed, gp_ref[...])


def _out_proj(attn, rec, x2, g_attn, g_rec, w, g_post, *, tm):
    t, d = x2.shape
    da, dr = attn.shape[1], rec.shape[1]
    est = 2 * tm * (da + dr) * 2 + 4 * tm * d * 4 + (da + dr) * d * 2 + tm * (da + dr) * 2
    row = lambda i: (i, 0)
    fixed = lambda i: (0, 0)
    return pl.pallas_call(
        _outproj_kernel,
        out_shape=jax.ShapeDtypeStruct((t, d), F32),
        grid=(t // tm,),
        in_specs=[pl.BlockSpec((tm, da), row), pl.BlockSpec((tm, dr), row), pl.BlockSpec((tm, d), row),
                  pl.BlockSpec((1, da), fixed), pl.BlockSpec((1, dr), fixed),
                  pl.BlockSpec((da + dr, d), fixed, pipeline_mode=pl.Buffered(1)),
                  pl.BlockSpec((1, d), fixed)],
        out_specs=pl.BlockSpec((tm, d), row),
        scratch_shapes=[pltpu.VMEM((tm, da + dr), BF16)],
        compiler_params=_params(("parallel",), est),
        name="out_proj",
    )(attn, rec, x2, g_attn, g_rec, w, g_post)


def _ffn_kernel(h_ref, gpre_ref, wg_ref, wu_ref, wd_ref, gpost_ref, o_ref, fn_ref, acc_ref, *, row_chunk):
    f = pl.program_id(1)

    @pl.when(f == 0)
    def _():
        def body(c, carry):
            r = pl.multiple_of(c * row_chunk, row_chunk)
            fn_ref[pl.ds(r, row_chunk), :] = _rms(h_ref[pl.ds(r, row_chunk), :], gpre_ref[...]).astype(BF16)
            return carry
        lax.fori_loop(0, h_ref.shape[0] // row_chunk, body, 0)

    fn = fn_ref[...]
    gate = jnp.dot(fn, wg_ref[...], preferred_element_type=F32)
    up = jnp.dot(fn, wu_ref[...], preferred_element_type=F32)
    act = (jax.nn.silu(gate) * up).astype(BF16)
    @pl.when(f == 0)
    def _():
        acc_ref[...] = jnp.dot(act, wd_ref[...], preferred_element_type=F32)

    @pl.when(f > 0)
    def _():
        acc_ref[...] += jnp.dot(act, wd_ref[...], preferred_element_type=F32)

    @pl.when(f == pl.num_programs(1) - 1)
    def _():
        def body(c, carry):
            r = pl.multiple_of(c * row_chunk, row_chunk)
            o_ref[pl.ds(r, row_chunk), :] = _rms(acc_ref[pl.ds(r, row_chunk), :], gpost_ref[...]).astype(o_ref.dtype)
            return carry
        lax.fori_loop(0, h_ref.shape[0] // row_chunk, body, 0)


def _ffn(h1, g_pre, w_gate, w_up, w_down, g_post, *, tm, tf):
    t, d = h1.shape
    dff = w_gate.shape[1]
    est = 2 * tm * d * 4 + 3 * 2 * d * tf * 2 + 2 * tm * d * 2 + tm * d * 2 + tm * d * 4
    return pl.pallas_call(
        functools.partial(_ffn_kernel, row_chunk=min(128, tm)),
        out_shape=jax.ShapeDtypeStruct((t, d), BF16),
        grid=(t // tm, dff // tf),
        in_specs=[pl.BlockSpec((tm, d), lambda i, f: (i, 0)),
                  pl.BlockSpec((1, d), lambda i, f: (0, 0)),
                  pl.BlockSpec((d, tf), lambda i, f: (0, f)),
                  pl.BlockSpec((d, tf), lambda i, f: (0, f)),
                  pl.BlockSpec((tf, d), lambda i, f: (f, 0)),
                  pl.BlockSpec((1, d), lambda i, f: (0, 0))],
        out_specs=pl.BlockSpec((tm, d), lambda i, f: (i, 0)),
        scratch_shapes=[pltpu.VMEM((tm, d), BF16), pltpu.VMEM((tm, d), F32)],
        compiler_params=_params(("parallel", "arbitrary"), est),
        name="ffn",
    )(h1, g_pre, w_gate, w_up, w_down, g_post)


def _ple_kernel(h_ref, d_ref, p_ref, gpre_ref, wg_ref, wp_ref, gpost_ref, o_ref):
    h2 = h_ref[...] + d_ref[...].astype(F32)
    gate = _sigmoid(jnp.dot(_rms(h2, gpre_ref[...]).astype(BF16), wg_ref[...],
                            preferred_element_type=F32))
    ple = jnp.dot(p_ref[...].astype(BF16), wp_ref[...], preferred_element_type=F32)
    o_ref[...] = h2 + _rms(gate * ple, gpost_ref[...])


def _ple(h1, dff, p2, g_pre, w_gate, w_proj, g_post, *, tm):
    t, d = h1.shape
    dp = p2.shape[1]
    est = 4 * tm * d * 4 + 2 * tm * d * 2 + 2 * tm * dp * 4 + d * d * 2 + 2 * dp * d * 2 + 3 * tm * d * 4
    row = lambda i: (i, 0)
    fixed = lambda i: (0, 0)
    return pl.pallas_call(
        _ple_kernel,
        out_shape=jax.ShapeDtypeStruct((t, d), F32),
        grid=(t // tm,),
        in_specs=[pl.BlockSpec((tm, d), row), pl.BlockSpec((tm, d), row), pl.BlockSpec((tm, dp), row),
                  pl.BlockSpec((1, d), fixed),
                  pl.BlockSpec((d, d), fixed, pipeline_mode=pl.Buffered(1)),
                  pl.BlockSpec((dp, d), fixed),
                  pl.BlockSpec((1, d), fixed)],
        out_specs=pl.BlockSpec((tm, d), row),
        compiler_params=_params(("parallel",), est),
        name="ple",
    )(h1, dff, p2, g_pre, w_gate, w_proj, g_post)


def _layer(h, p_i, g_mix_pre, w_in, rpb, conv_w, conv_b, w_rg_a, b_rg_a, w_rg_i, b_rg_i, lam,
           g_attn_out, g_rec_out, w_out, g_mix_post, g_ffn_pre, w_ffn_gate, w_ffn_up, w_ffn_down,
           g_ffn_post, g_ple_pre, w_ple_gate, w_ple_proj, g_ple_post):
    batch, seq, d = h.shape
    t = batch * seq
    d_attn = g_attn_out.shape[0]
    d_rec = g_rec_out.shape[0]
    nb = d_rec // REC_BLOCK_W
    row = lambda v: v.reshape(1, -1).astype(F32)

    x2 = h.reshape(t, d)
    u, (w_out_b, w_gate_b, w_up_b, w_down_b, w_pgate_b) = _in_proj(
        x2, row(g_mix_pre), w_in.astype(BF16), [w_out, w_ffn_gate, w_ffn_up, w_ffn_down, w_ple_gate],
        tm=min(512, t), tn=1024)

    attn = _attention(u, _attn_bias(rpb, seq // GRID_W), batch=batch, seq=seq, d_attn=d_attn)

    wg = (0.5 * jnp.concatenate([w_rg_a[0], w_rg_a[1], w_rg_i[0], w_rg_i[1]], axis=-1)).astype(BF16)
    bg = jnp.concatenate([b_rg_a.reshape(2, nb, 1, REC_BLOCK_W)[0], b_rg_a.reshape(2, nb, 1, REC_BLOCK_W)[1],
                          b_rg_i.reshape(2, nb, 1, REC_BLOCK_W)[0], b_rg_i.reshape(2, nb, 1, REC_BLOCK_W)[1]],
                         axis=-1).astype(F32) * 0.5
    rec = _rglru(u.reshape(batch, seq, -1), conv_w.astype(F32), row(conv_b), wg, bg, lam.astype(F32),
                 d_attn=d_attn, d_rec=d_rec)

    h1 = _out_proj(attn, rec.reshape(t, d_rec), x2, row(g_attn_out), row(g_rec_out),
                   w_out_b, row(g_mix_post), tm=min(512, t))

    dff = _ffn(h1, row(g_ffn_pre), w_gate_b, w_up_b,
               w_down_b, row(g_ffn_post), tm=min(1024, t), tf=512)

    out = _ple(h1, dff, p_i.reshape(t, -1), row(g_ple_pre), w_pgate_b,
               w_ple_proj.astype(BF16), row(g_ple_post), tm=min(512, t))
    return out.reshape(batch, seq, d)


def kernel(x, p, g_mix_pre, w_in, rpb, conv_w, conv_b, w_rg_a, b_rg_a, w_rg_i, b_rg_i, lam, g_attn_out, g_rec_out, w_out, g_mix_post, g_ffn_pre, w_ffn_gate, w_ffn_up, w_ffn_down, g_ffn_post, g_ple_pre, w_ple_gate, w_ple_proj, g_ple_post):
    h = x
    for i in range(p.shape[0]):
        h = _layer(h, p[i], g_mix_pre[i], w_in[i], rpb[i], conv_w[i], conv_b[i], w_rg_a[i], b_rg_a[i],
                   w_rg_i[i], b_rg_i[i], lam[i], g_attn_out[i], g_rec_out[i], w_out[i], g_mix_post[i],
                   g_ffn_pre[i], w_ffn_gate[i], w_ffn_up[i], w_ffn_down[i], g_ffn_post[i], g_ple_pre[i],
                   w_ple_gate[i], w_ple_proj[i], g_ple_post[i])
    return h
```

```python
import functools

import numpy as np
import jax
import jax.numpy as jnp
from jax import lax
from jax.experimental import pallas as pl
from jax.experimental.pallas import tpu as pltpu

F32 = jnp.float32
BF16 = jnp.bfloat16

EPS = 1e-6
NEG_INF = -1e9
LOG2_E = 1.4426950408889634
GRID_W = 64
HEAD_DIM = 128
WIN_R = 8
WIN_C = 16
CONV_W = 4
C_RG = 8.0
REC_BLOCK_W = 128

V7X_SUBLANES = 8
V7X_BF16_ROWS = 16
V7X_VMEM_BYTES = 64 * 1024 * 1024
V7X_VMEM_CAP = V7X_VMEM_BYTES - 8 * 1024 * 1024

Q_ROWS = 4
Q_TOK = Q_ROWS * GRID_W
K_BLOCKS = 3

SCAN_CHUNK = 256
GATE_UNROLL = 4
SCAN_UNROLL = 8


def _vmem_limit(estimate_bytes):
    return int(min(estimate_bytes * 5 // 4 + (4 << 20), V7X_VMEM_CAP))


def _params(semantics, estimate_bytes):
    return pltpu.CompilerParams(dimension_semantics=semantics,
                                vmem_limit_bytes=_vmem_limit(estimate_bytes))


def _rms(x, g):
    ms = jnp.mean(x * x, axis=-1, keepdims=True)
    return x * lax.rsqrt(ms + EPS) * g


def _inproj_kernel(*refs, tn, n_riders):
    x_ref, g_ref, w_ref = refs[:3]
    rider_src = refs[3:3 + n_riders]
    o_ref, slab_ref = refs[3 + n_riders:5 + n_riders]
    rider_dst = refs[5 + n_riders:5 + 2 * n_riders]
    hn_ref = refs[5 + 2 * n_riders]
    for src, dst in zip(rider_src, rider_dst):
        dst[...] = src[...].astype(dst.dtype)
    hn_ref[...] = _rms(x_ref[...], g_ref[...]).astype(BF16)
    n_main = o_ref.shape[1]
    w = REC_BLOCK_W
    for c in range(0, w_ref.shape[1], tn):
        res = jnp.dot(hn_ref[...], w_ref[:, c:c + tn], preferred_element_type=F32).astype(o_ref.dtype)
        if c < n_main:
            o_ref[:, c:c + tn] = res
        else:
            for k in range(tn // w):
                slab_ref[(c - n_main) // w + k] = res[:, k * w:(k + 1) * w]


def _rider_block(rows, steps):
    share = 1
    while (rows * share) % (steps * V7X_BF16_ROWS) != 0:
        share *= 2
    return rows * share // steps, share


def _in_proj(x2, g, w, riders, *, n_main, tm, tn):
    t, d = x2.shape
    n = w.shape[1]
    assert n_main % tn == 0 and (n - n_main) % tn == 0 and tn % REC_BLOCK_W == 0
    n_slabs = (n - n_main) // REC_BLOCK_W
    steps = t // tm
    est = 2 * tm * d * 4 + d * n * 2 + 2 * tm * n * 2 + tm * d * 2 + tm * d * 4
    rider_specs = []
    for r in riders:
        rows, share = _rider_block(r.shape[0], steps)
        rider_specs.append(pl.BlockSpec((rows, r.shape[1]), lambda i, share=share: (i // share, 0)))
        est += 2 * rows * r.shape[1] * (4 + 2)
    out = pl.pallas_call(
        functools.partial(_inproj_kernel, tn=tn, n_riders=len(riders)),
        out_shape=[jax.ShapeDtypeStruct((t, n_main), BF16),
                   jax.ShapeDtypeStruct((n_slabs, t, REC_BLOCK_W), BF16)]
                  + [jax.ShapeDtypeStruct(r.shape, BF16) for r in riders],
        grid=(steps,),
        in_specs=[pl.BlockSpec((tm, d), lambda i: (i, 0)),
                  pl.BlockSpec((1, d), lambda i: (0, 0)),
                  pl.BlockSpec((d, n), lambda i: (0, 0), pipeline_mode=pl.Buffered(1))] + rider_specs,
        out_specs=[pl.BlockSpec((tm, n_main), lambda i: (i, 0)),
                   pl.BlockSpec((n_slabs, tm, REC_BLOCK_W), lambda i: (0, i, 0))] + rider_specs,
        scratch_shapes=[pltpu.VMEM((tm, d), BF16)],
        compiler_params=_params(("arbitrary",), est),
        name="in_proj",
    )(x2, g, w, *riders)
    return out[0], out[1], out[2:]


def _attn_window(rows):
    nblk = rows // Q_ROWS
    kr = min(WIN_R, rows)
    n_dr = 2 * WIN_R - 1
    q_row0 = np.array([0, Q_ROWS, rows - Q_ROWS])
    k_row0 = np.array([0, 0, (nblk - K_BLOCKS) * Q_ROWS])
    qr = q_row0[:, None] + np.arange(Q_ROWS)[None, :]
    kro = k_row0[:, None] + np.arange(K_BLOCKS * Q_ROWS)[None, :]
    rstart = np.clip(qr - kr // 2, 0, rows - kr)
    row_ok = (kro[:, None, :] >= rstart[:, :, None]) & (kro[:, None, :] < rstart[:, :, None] + kr)
    dr = np.clip(kro[:, None, :] - qr[:, :, None] + WIN_R - 1, 0, n_dr - 1)
    return row_ok, dr


def _bias_kernel(rpb_ref, o_ref, *, row_ok, dr):
    lanes = 2 * GRID_W
    qc = lax.broadcasted_iota(jnp.int32, (GRID_W, lanes), 0)
    lane = lax.broadcasted_iota(jnp.int32, (GRID_W, lanes), 1)
    kc = lane & (GRID_W - 1)
    cstart = jnp.clip(qc - WIN_C // 2, 0, GRID_W - WIN_C)
    col_ok = (kc >= cstart) & (kc < cstart + WIN_C)
    first = lane < GRID_W
    neg = jnp.full((GRID_W, lanes), NEG_INF, F32)

    toeplitz = {}

    def block(r, second):
        if (r, second) not in toeplitz:
            row = jnp.broadcast_to(rpb_ref[r:r + 1, :], (GRID_W, lanes))
            t = pltpu.roll(row, lanes - (WIN_C - 1), 1, stride=1, stride_axis=0)
            toeplitz[(r, False)] = t
            toeplitz[(r, True)] = pltpu.roll(t, GRID_W, 1)
        return toeplitz[(r, second)]

    for t in range(3):
        for a in range(Q_ROWS):
            for pair in range(K_BLOCKS * Q_ROWS // 2):
                l0, l1 = 2 * pair, 2 * pair + 1
                left = block(int(dr[t, a, l0]), False) if row_ok[t, a, l0] else neg
                right = block(int(dr[t, a, l1]), True) if row_ok[t, a, l1] else neg
                tile = jnp.where(col_ok, jnp.where(first, left, right), neg)
                o_ref[t, a * GRID_W:(a + 1) * GRID_W, pair * lanes:(pair + 1) * lanes] = tile


def _attn_bias(rpb, rows):
    n_heads, n_dr, n_dc = rpb.shape
    row_ok, dr = _attn_window(rows)
    padded = jnp.zeros((n_heads, -(-n_dr // V7X_SUBLANES) * V7X_SUBLANES, 2 * GRID_W), F32)
    padded = padded.at[:, :n_dr, :n_dc].set(rpb.astype(F32))
    shape = (3, n_heads, Q_TOK, K_BLOCKS * Q_TOK)
    return pl.pallas_call(
        functools.partial(_bias_kernel, row_ok=row_ok, dr=dr),
        out_shape=jax.ShapeDtypeStruct(shape, F32),
        grid=(n_heads,),
        in_specs=[pl.BlockSpec((None,) + padded.shape[1:], lambda h: (h, 0, 0))],
        out_specs=pl.BlockSpec((3, None) + shape[2:], lambda h: (0, h, 0, 0)),
        compiler_params=_params(("parallel",), 2 * 3 * shape[2] * shape[3] * 4),
        name="attn_bias",
    )(padded)


def _attn_heads(q_ref, k_refs, v_refs, bias_ref, o_ref, *, n_heads, need):
    scale = HEAD_DIM ** -0.5
    n_tiles = len(need[0])
    per_blk = n_tiles // K_BLOCKS
    zeros = jnp.zeros((GRID_W, HEAD_DIM), BF16)
    for h in range(n_heads):
        hs = slice(h * HEAD_DIM, (h + 1) * HEAD_DIM)
        q = (q_ref[:, hs].astype(F32) * scale).astype(BF16)
        s = [lax.dot_general(q, k_refs[i][:, hs], (((1,), (1,)), ((), ())),
                             preferred_element_type=F32) for i in range(K_BLOCKS)]
        p_rows, l_rows = [], []
        for a in range(Q_ROWS):
            rs = slice(a * GRID_W, (a + 1) * GRID_W)
            tiles = {t: s[t // per_blk][rs, (t % per_blk) * HEAD_DIM:(t % per_blk + 1) * HEAD_DIM]
                        + bias_ref[h, rs, t * HEAD_DIM:(t + 1) * HEAD_DIM]
                     for t in range(n_tiles) if need[a][t]}
            m = functools.reduce(jnp.maximum, tiles.values()).max(axis=-1, keepdims=True)
            p = {t: jnp.exp(v - m) for t, v in tiles.items()}
            l_rows.append(functools.reduce(jnp.add, p.values()).sum(axis=-1, keepdims=True))
            p_rows.append([p[t].astype(BF16) if t in p else zeros for t in range(n_tiles)])
        acc = None
        for i in range(K_BLOCKS):
            p_blk = jnp.concatenate(
                [jnp.concatenate(row[i * per_blk:(i + 1) * per_blk], axis=1) for row in p_rows], axis=0)
            part = jnp.dot(p_blk, v_refs[i][:, hs], preferred_element_type=F32)
            acc = part if acc is None else acc + part
        o_ref[:, hs] = (acc * (1.0 / jnp.concatenate(l_rows, axis=0))).astype(o_ref.dtype)


def _attn_kernel(q_ref, k0_ref, k1_ref, k2_ref, v0_ref, v1_ref, v2_ref, bias_ref, o_ref, *,
                 n_heads, need):
    j = pl.program_id(0)
    last = pl.num_programs(0) - 1
    block_type = jnp.where(j == 0, 0, jnp.where(j == last, 2, 1))
    for t in range(3):
        @pl.when(block_type == t)
        def _():
            _attn_heads(q_ref, (k0_ref, k1_ref, k2_ref), (v0_ref, v1_ref, v2_ref), bias_ref, o_ref,
                        n_heads=n_heads, need=need[t])


def _attention(u, bias, *, batch, seq, d_attn):
    t = u.shape[0]
    row_ok, _ = _attn_window(seq // GRID_W)
    pairs = row_ok.reshape(3, Q_ROWS, -1, HEAD_DIM // GRID_W)
    need = tuple(tuple(tuple(bool(x) for x in row) for row in typ) for typ in pairs.any(axis=-1))
    n_heads = d_attn // HEAD_DIM
    nblk = seq // Q_TOK
    assert nblk >= K_BLOCKS

    def q_map(j, b):
        return (b * nblk + j, 0)

    def kv_map(i, col):
        def f(j, b):
            return (b * nblk + jnp.clip(j - 1, 0, nblk - K_BLOCKS) + i, col)
        return f

    def bias_map(j, b):
        return (jnp.where(j == 0, 0, jnp.where(j == nblk - 1, 2, 1)), 0, 0, 0)

    blk = pl.BlockSpec((Q_TOK, d_attn), q_map)
    in_specs = [blk]
    in_specs += [pl.BlockSpec((Q_TOK, d_attn), kv_map(i, 1)) for i in range(K_BLOCKS)]
    in_specs += [pl.BlockSpec((Q_TOK, d_attn), kv_map(i, 2)) for i in range(K_BLOCKS)]
    in_specs += [pl.BlockSpec((None, n_heads, Q_TOK, K_BLOCKS * Q_TOK), bias_map)]
    est = 2 * 8 * Q_TOK * d_attn * 2 + 2 * n_heads * Q_TOK * K_BLOCKS * Q_TOK * 4
    return pl.pallas_call(
        functools.partial(_attn_kernel, n_heads=n_heads, need=need),
        out_shape=jax.ShapeDtypeStruct((t, d_attn), BF16),
        grid=(nblk, batch),
        in_specs=in_specs,
        out_specs=blk,
        compiler_params=_params(("arbitrary", "arbitrary"), est),
        name="nattn",
    )(u, u, u, u, u, u, u, bias)


def _scan_pitch(n_chunks):
    groups = n_chunks // V7X_SUBLANES
    return (groups + 1 - groups % 2) * V7X_SUBLANES


def _sigmoid(x):
    return 0.5 * jnp.tanh(0.5 * x) + 0.5


def _rglru_kernel(xr_ref, yg_ref, cw_ref, cb_ref, wg_ref, bg_ref, lam_ref, o_ref,
                  xp_ref, a_ref, b_ref, h_ref, cr_ref, *, seq):
    n_chunks = seq // SCAN_CHUNK
    pitch = _scan_pitch(n_chunks)
    w = REC_BLOCK_W
    halo = V7X_BF16_ROWS

    zeros = jnp.zeros((halo, w), xp_ref.dtype)
    xp_ref[0:halo, :] = zeros
    xp_ref[seq + halo:seq + 2 * halo, :] = zeros

    def stage(c, carry):
        r = pl.multiple_of(c * SCAN_CHUNK, SCAN_CHUNK)
        xp_ref[pl.ds(r + halo, SCAN_CHUNK), :] = xr_ref[pl.ds(r, SCAN_CHUNK), :].astype(F32)
        return carry
    lax.fori_loop(0, n_chunks, stage, 0)

    z = -lam_ref[...]
    half_coef = (-0.5 * C_RG * LOG2_E) * (jnp.maximum(z, 0.0) + jnp.log1p(jnp.exp(-jnp.abs(z))))

    left = CONV_W // 2

    def gates(c, carry):
        r = pl.multiple_of(c * SCAN_CHUNK, SCAN_CHUNK)
        xc = cb_ref[...]
        for j in range(CONV_W):
            xc = xc + xp_ref[pl.ds(r + (halo + j - left), SCAN_CHUNK), :] * cw_ref[j:j + 1, :]
        th = jnp.tanh(jnp.dot(xc.astype(BF16), wg_ref[...], preferred_element_type=F32) + bg_ref[...])
        xh = 0.5 * xc
        chunk_rows = pl.ds(c, SCAN_CHUNK, stride=pitch)
        for d in range(2):
            a = jnp.exp2(th[:, d * w:(d + 1) * w] * half_coef[d:d + 1, :] + half_coef[d:d + 1, :])
            y = 1.0 - a * a
            root = jnp.where(y > 0.0, y * lax.rsqrt(y), 0.0)
            a_ref[d, chunk_rows, :] = a
            b_ref[d, chunk_rows, :] = root * (th[:, (2 + d) * w:(3 + d) * w] + 1.0) * xh
        return carry
    lax.fori_loop(0, n_chunks, gates, 0, unroll=GATE_UNROLL)

    def step_rows(d, t):
        pos = t if d == 0 else SCAN_CHUNK - 1 - t
        return pl.ds(pl.multiple_of(pos * pitch, V7X_SUBLANES), n_chunks)

    def pass1(t, carry):
        hs, ps = carry
        a = [a_ref[d, step_rows(d, t), :] for d in range(2)]
        return (tuple(a[d] * hs[d] + b_ref[d, step_rows(d, t), :] for d in range(2)),
                tuple(a[d] * ps[d] for d in range(2)))

    zero = jnp.zeros((n_chunks, w), F32)
    he, pe = lax.fori_loop(0, SCAN_CHUNK, pass1, ((zero, zero), (zero + 1.0, zero + 1.0)),
                           unroll=SCAN_UNROLL)

    for d in range(2):
        order = range(n_chunks) if d == 0 else range(n_chunks - 1, -1, -1)
        carry = jnp.zeros((1, w), F32)
        for j in order:
            cr_ref[d, j:j + 1, :] = carry
            carry = he[d][j:j + 1, :] + pe[d][j:j + 1, :] * carry

    def pass2(t, hs):
        nh = []
        for d in range(2):
            h = a_ref[d, step_rows(d, t), :] * hs[d] + b_ref[d, step_rows(d, t), :]
            h_ref[d, step_rows(d, t), :] = h
            nh.append(h)
        return tuple(nh)

    lax.fori_loop(0, SCAN_CHUNK, pass2, (cr_ref[0], cr_ref[1]), unroll=SCAN_UNROLL)

    def combine(c, carry):
        r = pl.multiple_of(c * SCAN_CHUNK, SCAN_CHUNK)
        chunk_rows = pl.ds(c, SCAN_CHUNK, stride=pitch)
        h = h_ref[0, chunk_rows, :] + h_ref[1, chunk_rows, :]
        y = yg_ref[pl.ds(r, SCAN_CHUNK), :].astype(F32)
        o_ref[pl.ds(r, SCAN_CHUNK), :] = (h * jax.nn.gelu(y)).astype(o_ref.dtype)
        return carry
    lax.fori_loop(0, n_chunks, combine, 0)


def _rglru(slabs, conv_w, conv_b, wg, bg, lam, *, d_rec):
    _, b, s, w = slabs.shape
    nb = d_rec // w
    assert s % (SCAN_CHUNK * V7X_SUBLANES) == 0
    n_chunks = s // SCAN_CHUNK
    scan_buf = pltpu.VMEM((2, SCAN_CHUNK * _scan_pitch(n_chunks), w), F32)
    est = (3 * 2 * s * w * 2 + (s + 2 * V7X_BF16_ROWS) * w * 4
           + 3 * 2 * SCAN_CHUNK * _scan_pitch(n_chunks) * w * 4)
    return pl.pallas_call(
        functools.partial(_rglru_kernel, seq=s),
        out_shape=jax.ShapeDtypeStruct((b, s, d_rec), BF16),
        grid=(b, nb),
        in_specs=[pl.BlockSpec((None, None, s, w), lambda i, n: (n, i, 0, 0)),
                  pl.BlockSpec((None, None, s, w), lambda i, n: (nb + n, i, 0, 0)),
                  pl.BlockSpec((CONV_W, w), lambda i, n: (0, n)),
                  pl.BlockSpec((1, w), lambda i, n: (0, n)),
                  pl.BlockSpec((None, w, 4 * w), lambda i, n: (n, 0, 0)),
                  pl.BlockSpec((None, 1, 4 * w), lambda i, n: (n, 0, 0)),
                  pl.BlockSpec((2, w), lambda i, n: (0, n))],
        out_specs=pl.BlockSpec((None, s, w), lambda i, n: (i, 0, n)),
        scratch_shapes=[pltpu.VMEM((s + 2 * V7X_BF16_ROWS, w), F32),
                        scan_buf, scan_buf, scan_buf, pltpu.VMEM((2, n_chunks, w), F32)],
        compiler_params=_params(("parallel", "arbitrary"), est),
        name="rglru",
    )(slabs, slabs, conv_w, conv_b, wg, bg, lam)


def _outproj_kernel(a_ref, r_ref, x_ref, ga_ref, gr_ref, w_ref, gp_ref, o_ref, cat_ref):
    da = a_ref.shape[1]
    cat_ref[:, :da] = _rms(a_ref[...].astype(F32), ga_ref[...]).astype(BF16)
    cat_ref[:, da:] = _rms(r_ref[...].astype(F32), gr_ref[...]).astype(BF16)
    mixed = jnp.dot(cat_ref[...], w_ref[...], preferred_element_type=F32)
    o_ref[...] = x_ref[...] + _rms(mixed, gp_ref[...])


def _out_proj(attn, rec, x2, g_attn, g_rec, w, g_post, *, tm):
    t, d = x2.shape
    da, dr = attn.shape[1], rec.shape[1]
    est = 2 * tm * (da + dr) * 2 + 4 * tm * d * 4 + (da + dr) * d * 2 + tm * (da + dr) * 2
    row = lambda i: (i, 0)
    fixed = lambda i: (0, 0)
    return pl.pallas_call(
        _outproj_kernel,
        out_shape=jax.ShapeDtypeStruct((t, d), F32),
        grid=(t // tm,),
        in_specs=[pl.BlockSpec((tm, da), row), pl.BlockSpec((tm, dr), row), pl.BlockSpec((tm, d), row),
                  pl.BlockSpec((1, da), fixed), pl.BlockSpec((1, dr), fixed),
                  pl.BlockSpec((da + dr, d), fixed, pipeline_mode=pl.Buffered(1)),
                  pl.BlockSpec((1, d), fixed)],
        out_specs=pl.BlockSpec((tm, d), row),
        scratch_shapes=[pltpu.VMEM((tm, da + dr), BF16)],
        compiler_params=_params(("parallel",), est),
        name="out_proj",
    )(attn, rec, x2, g_attn, g_rec, w, g_post)


def _ffn_kernel(h_ref, gpre_ref, wg_ref, wu_ref, wd_ref, gpost_ref, o_ref, fn_ref, acc_ref, *, row_chunk):
    f = pl.program_id(1)

    @pl.when(f == 0)
    def _():
        def body(c, carry):
            r = pl.multiple_of(c * row_chunk, row_chunk)
            fn_ref[pl.ds(r, row_chunk), :] = _rms(h_ref[pl.ds(r, row_chunk), :], gpre_ref[...]).astype(BF16)
            return carry
        lax.fori_loop(0, h_ref.shape[0] // row_chunk, body, 0)

    fn = fn_ref[...]
    gate = jnp.dot(fn, wg_ref[...], preferred_element_type=F32)
    up = jnp.dot(fn, wu_ref[...], preferred_element_type=F32)
    act = (jax.nn.silu(gate) * up).astype(BF16)
    @pl.when(f == 0)
    def _():
        acc_ref[...] = jnp.dot(act, wd_ref[...], preferred_element_type=F32)

    @pl.when(f > 0)
    def _():
        acc_ref[...] += jnp.dot(act, wd_ref[...], preferred_element_type=F32)

    @pl.when(f == pl.num_programs(1) - 1)
    def _():
        def body(c, carry):
            r = pl.multiple_of(c * row_chunk, row_chunk)
            o_ref[pl.ds(r, row_chunk), :] = _rms(acc_ref[pl.ds(r, row_chunk), :], gpost_ref[...]).astype(o_ref.dtype)
            return carry
        lax.fori_loop(0, h_ref.shape[0] // row_chunk, body, 0)


def _ffn(h1, g_pre, w_gate, w_up, w_down, g_post, *, tm, tf):
    t, d = h1.shape
    dff = w_gate.shape[1]
    est = 2 * tm * d * 4 + 3 * 2 * d * tf * 2 + 2 * tm * d * 2 + tm * d * 2 + tm * d * 4
    return pl.pallas_call(
        functools.partial(_ffn_kernel, row_chunk=min(128, tm)),
        out_shape=jax.ShapeDtypeStruct((t, d), BF16),
        grid=(t // tm, dff // tf),
        in_specs=[pl.BlockSpec((tm, d), lambda i, f: (i, 0)),
                  pl.BlockSpec((1, d), lambda i, f: (0, 0)),
                  pl.BlockSpec((d, tf), lambda i, f: (0, f)),
                  pl.BlockSpec((d, tf), lambda i, f: (0, f)),
                  pl.BlockSpec((tf, d), lambda i, f: (f, 0)),
                  pl.BlockSpec((1, d), lambda i, f: (0, 0))],
        out_specs=pl.BlockSpec((tm, d), lambda i, f: (i, 0)),
        scratch_shapes=[pltpu.VMEM((tm, d), BF16), pltpu.VMEM((tm, d), F32)],
        compiler_params=_params(("parallel", "arbitrary"), est),
        name="ffn",
    )(h1, g_pre, w_gate, w_up, w_down, g_post)


def _ple_kernel(h_ref, d_ref, p_ref, gpre_ref, wg_ref, wp_ref, gpost_ref, o_ref):
    h2 = h_ref[...] + d_ref[...].astype(F32)
    gate = _sigmoid(jnp.dot(_rms(h2, gpre_ref[...]).astype(BF16), wg_ref[...],
                            preferred_element_type=F32))
    ple = jnp.dot(p_ref[...].astype(BF16), wp_ref[...], preferred_element_type=F32)
    o_ref[...] = h2 + _rms(gate * ple, gpost_ref[...])


def _ple(h1, dff, p2, g_pre, w_gate, w_proj, g_post, *, tm):
    t, d = h1.shape
    dp = p2.shape[1]
    est = 4 * tm * d * 4 + 2 * tm * d * 2 + 2 * tm * dp * 4 + d * d * 2 + 2 * dp * d * 2 + 3 * tm * d * 4
    row = lambda i: (i, 0)
    fixed = lambda i: (0, 0)
    return pl.pallas_call(
        _ple_kernel,
        out_shape=jax.ShapeDtypeStruct((t, d), F32),
        grid=(t // tm,),
        in_specs=[pl.BlockSpec((tm, d), row), pl.BlockSpec((tm, d), row), pl.BlockSpec((tm, dp), row),
                  pl.BlockSpec((1, d), fixed),
                  pl.BlockSpec((d, d), fixed, pipeline_mode=pl.Buffered(1)),
                  pl.BlockSpec((dp, d), fixed),
                  pl.BlockSpec((1, d), fixed)],
        out_specs=pl.BlockSpec((tm, d), row),
        compiler_params=_params(("parallel",), est),
        name="ple",
    )(h1, dff, p2, g_pre, w_gate, w_proj, g_post)


def _layer(h, p_i, g_mix_pre, w_in, rpb, conv_w, conv_b, w_rg_a, b_rg_a, w_rg_i, b_rg_i, lam,
           g_attn_out, g_rec_out, w_out, g_mix_post, g_ffn_pre, w_ffn_gate, w_ffn_up, w_ffn_down,
           g_ffn_post, g_ple_pre, w_ple_gate, w_ple_proj, g_ple_post):
    batch, seq, d = h.shape
    t = batch * seq
    d_attn = g_attn_out.shape[0]
    d_rec = g_rec_out.shape[0]
    nb = d_rec // REC_BLOCK_W
    row = lambda v: v.reshape(1, -1).astype(F32)

    x2 = h.reshape(t, d)
    u, u_rec, (w_out_b, w_gate_b, w_up_b, w_down_b, w_pgate_b) = _in_proj(
        x2, row(g_mix_pre), w_in.astype(BF16), [w_out, w_ffn_gate, w_ffn_up, w_ffn_down, w_ple_gate],
        n_main=3 * d_attn, tm=min(512, t), tn=1024)

    attn = _attention(u, _attn_bias(rpb, seq // GRID_W), batch=batch, seq=seq, d_attn=d_attn)

    wg = (0.5 * jnp.concatenate([w_rg_a[0], w_rg_a[1], w_rg_i[0], w_rg_i[1]], axis=-1)).astype(BF16)
    bg = jnp.concatenate([b_rg_a.reshape(2, nb, 1, REC_BLOCK_W)[0], b_rg_a.reshape(2, nb, 1, REC_BLOCK_W)[1],
                          b_rg_i.reshape(2, nb, 1, REC_BLOCK_W)[0], b_rg_i.reshape(2, nb, 1, REC_BLOCK_W)[1]],
                         axis=-1).astype(F32) * 0.5
    rec = _rglru(u_rec.reshape(-1, batch, seq, REC_BLOCK_W), conv_w.astype(F32), row(conv_b), wg, bg,
                 lam.astype(F32), d_rec=d_rec)

    h1 = _out_proj(attn, rec.reshape(t, d_rec), x2, row(g_attn_out), row(g_rec_out),
                   w_out_b, row(g_mix_post), tm=min(512, t))

    dff = _ffn(h1, row(g_ffn_pre), w_gate_b, w_up_b,
               w_down_b, row(g_ffn_post), tm=min(1024, t), tf=512)

    out = _ple(h1, dff, p_i.reshape(t, -1), row(g_ple_pre), w_pgate_b,
               w_ple_proj.astype(BF16), row(g_ple_post), tm=min(512, t))
    return out.reshape(batch, seq, d)


def kernel(x, p, g_mix_pre, w_in, rpb, conv_w, conv_b, w_rg_a, b_rg_a, w_rg_i, b_rg_i, lam, g_attn_out, g_rec_out, w_out, g_mix_post, g_ffn_pre, w_ffn_gate, w_ffn_up, w_ffn_down, g_ffn_post, g_ple_pre, w_ple_gate, w_ple_proj, g_ple_post):
    h = x
    for i in range(p.shape[0]):
        h = _layer(h, p[i], g_mix_pre[i], w_in[i], rpb[i], conv_w[i], conv_b[i], w_rg_a[i], b_rg_a[i],
                   w_rg_i[i], b_rg_i[i], lam[i], g_attn_out[i], g_rec_out[i], w_out[i], g_mix_post[i],
                   g_ffn_pre[i], w_ffn_gate[i], w_ffn_up[i], w_ffn_down[i], g_ffn_post[i], g_ple_pre[i],
                   w_ple_gate[i], w_ple_proj[i], g_ple_post[i])
    return h
```

```python
import functools

import numpy as np
import jax
import jax.numpy as jnp
from jax import lax
from jax.experimental import pallas as pl
from jax.experimental.pallas import tpu as pltpu

F32 = jnp.float32
BF16 = jnp.bfloat16

EPS = 1e-6
NEG_INF = -1e9
LOG2_E = 1.4426950408889634
GRID_W = 64
HEAD_DIM = 128
WIN_R = 8
WIN_C = 16
CONV_W = 4
C_RG = 8.0
REC_BLOCK_W = 128

V7X_SUBLANES = 8
V7X_BF16_ROWS = 16
V7X_VMEM_BYTES = 64 * 1024 * 1024
V7X_VMEM_CAP = V7X_VMEM_BYTES - 8 * 1024 * 1024

Q_ROWS = 4
Q_TOK = Q_ROWS * GRID_W
K_BLOCKS = 3

SCAN_CHUNK = 256
GATE_UNROLL = 4
SCAN_UNROLL = 8


def _vmem_limit(estimate_bytes):
    return int(min(estimate_bytes * 5 // 4 + (4 << 20), V7X_VMEM_CAP))


def _params(semantics, estimate_bytes):
    return pltpu.CompilerParams(dimension_semantics=semantics,
                                vmem_limit_bytes=_vmem_limit(estimate_bytes))


def _rms(x, g):
    ms = jnp.mean(x * x, axis=-1, keepdims=True)
    return x * lax.rsqrt(ms + EPS) * g


def _inproj_kernel(*refs, tn, n_riders):
    x_ref, g_ref, w_ref = refs[:3]
    rider_src = refs[3:3 + n_riders]
    o_ref, slab_ref = refs[3 + n_riders:5 + n_riders]
    rider_dst = refs[5 + n_riders:5 + 2 * n_riders]
    hn_ref = refs[5 + 2 * n_riders]
    for src, dst in zip(rider_src, rider_dst):
        dst[...] = src[...].astype(dst.dtype)
    hn_ref[...] = _rms(x_ref[...], g_ref[...]).astype(BF16)
    n_main = o_ref.shape[1]
    w = REC_BLOCK_W
    for c in range(0, w_ref.shape[1], tn):
        res = jnp.dot(hn_ref[...], w_ref[:, c:c + tn], preferred_element_type=F32).astype(o_ref.dtype)
        if c < n_main:
            o_ref[:, c:c + tn] = res
        else:
            for k in range(tn // w):
                slab_ref[(c - n_main) // w + k] = res[:, k * w:(k + 1) * w]


def _rider_block(rows, steps):
    share = 1
    while (rows * share) % (steps * V7X_BF16_ROWS) != 0:
        share *= 2
    return rows * share // steps, share


def _in_proj(x2, g, w, riders, *, n_main, tm, tn):
    t, d = x2.shape
    n = w.shape[1]
    assert n_main % tn == 0 and (n - n_main) % tn == 0 and tn % REC_BLOCK_W == 0
    n_slabs = (n - n_main) // REC_BLOCK_W
    steps = t // tm
    est = 2 * tm * d * 4 + d * n * 2 + 2 * tm * n * 2 + tm * d * 2 + tm * d * 4
    rider_specs = []
    for r in riders:
        rows, share = _rider_block(r.shape[0], steps)
        rider_specs.append(pl.BlockSpec((rows, r.shape[1]), lambda i, share=share: (i // share, 0)))
        est += 2 * rows * r.shape[1] * (4 + 2)
    out = pl.pallas_call(
        functools.partial(_inproj_kernel, tn=tn, n_riders=len(riders)),
        out_shape=[jax.ShapeDtypeStruct((t, n_main), BF16),
                   jax.ShapeDtypeStruct((n_slabs, t, REC_BLOCK_W), BF16)]
                  + [jax.ShapeDtypeStruct(r.shape, BF16) for r in riders],
        grid=(steps,),
        in_specs=[pl.BlockSpec((tm, d), lambda i: (i, 0)),
                  pl.BlockSpec((1, d), lambda i: (0, 0)),
                  pl.BlockSpec((d, n), lambda i: (0, 0), pipeline_mode=pl.Buffered(1))] + rider_specs,
        out_specs=[pl.BlockSpec((tm, n_main), lambda i: (i, 0)),
                   pl.BlockSpec((n_slabs, tm, REC_BLOCK_W), lambda i: (0, i, 0))] + rider_specs,
        scratch_shapes=[pltpu.VMEM((tm, d), BF16)],
        compiler_params=_params(("arbitrary",), est),
        name="in_proj",
    )(x2, g, w, *riders)
    return out[0], out[1], out[2:]


def _attn_window(rows):
    nblk = rows // Q_ROWS
    kr = min(WIN_R, rows)
    n_dr = 2 * WIN_R - 1
    q_row0 = np.array([0, Q_ROWS, rows - Q_ROWS])
    k_row0 = np.array([0, 0, (nblk - K_BLOCKS) * Q_ROWS])
    qr = q_row0[:, None] + np.arange(Q_ROWS)[None, :]
    kro = k_row0[:, None] + np.arange(K_BLOCKS * Q_ROWS)[None, :]
    rstart = np.clip(qr - kr // 2, 0, rows - kr)
    row_ok = (kro[:, None, :] >= rstart[:, :, None]) & (kro[:, None, :] < rstart[:, :, None] + kr)
    dr = np.clip(kro[:, None, :] - qr[:, :, None] + WIN_R - 1, 0, n_dr - 1)
    return row_ok, dr


def _bias_kernel(rpb_ref, o_ref, *, row_ok, dr):
    lanes = 2 * GRID_W
    qc = lax.broadcasted_iota(jnp.int32, (GRID_W, lanes), 0)
    lane = lax.broadcasted_iota(jnp.int32, (GRID_W, lanes), 1)
    kc = lane & (GRID_W - 1)
    cstart = jnp.clip(qc - WIN_C // 2, 0, GRID_W - WIN_C)
    col_ok = (kc >= cstart) & (kc < cstart + WIN_C)
    first = lane < GRID_W
    neg = jnp.full((GRID_W, lanes), NEG_INF, F32)

    toeplitz = {}

    def block(r, second):
        if (r, second) not in toeplitz:
            row = jnp.broadcast_to(rpb_ref[r:r + 1, :], (GRID_W, lanes))
            t = pltpu.roll(row, lanes - (WIN_C - 1), 1, stride=1, stride_axis=0)
            toeplitz[(r, False)] = t
            toeplitz[(r, True)] = pltpu.roll(t, GRID_W, 1)
        return toeplitz[(r, second)]

    for t in range(3):
        for a in range(Q_ROWS):
            for pair in range(K_BLOCKS * Q_ROWS // 2):
                l0, l1 = 2 * pair, 2 * pair + 1
                left = block(int(dr[t, a, l0]), False) if row_ok[t, a, l0] else neg
                right = block(int(dr[t, a, l1]), True) if row_ok[t, a, l1] else neg
                tile = jnp.where(col_ok, jnp.where(first, left, right), neg)
                o_ref[t, a * GRID_W:(a + 1) * GRID_W, pair * lanes:(pair + 1) * lanes] = tile


def _attn_bias(rpb, rows):
    n_heads, n_dr, n_dc = rpb.shape
    row_ok, dr = _attn_window(rows)
    padded = jnp.zeros((n_heads, -(-n_dr // V7X_SUBLANES) * V7X_SUBLANES, 2 * GRID_W), F32)
    padded = padded.at[:, :n_dr, :n_dc].set(rpb.astype(F32))
    shape = (3, n_heads, Q_TOK, K_BLOCKS * Q_TOK)
    return pl.pallas_call(
        functools.partial(_bias_kernel, row_ok=row_ok, dr=dr),
        out_shape=jax.ShapeDtypeStruct(shape, F32),
        grid=(n_heads,),
        in_specs=[pl.BlockSpec((None,) + padded.shape[1:], lambda h: (h, 0, 0))],
        out_specs=pl.BlockSpec((3, None) + shape[2:], lambda h: (0, h, 0, 0)),
        compiler_params=_params(("parallel",), 2 * 3 * shape[2] * shape[3] * 4),
        name="attn_bias",
    )(padded)


def _attn_heads(q_ref, k_refs, v_refs, bias_ref, o_ref, *, n_heads, need):
    scale = HEAD_DIM ** -0.5
    n_tiles = len(need[0])
    per_blk = n_tiles // K_BLOCKS
    zeros = jnp.zeros((GRID_W, HEAD_DIM), BF16)
    for h in range(n_heads):
        hs = slice(h * HEAD_DIM, (h + 1) * HEAD_DIM)
        q = (q_ref[:, hs].astype(F32) * scale).astype(BF16)
        s = [lax.dot_general(q, k_refs[i][:, hs], (((1,), (1,)), ((), ())),
                             preferred_element_type=F32) for i in range(K_BLOCKS)]
        p_rows, l_rows = [], []
        for a in range(Q_ROWS):
            rs = slice(a * GRID_W, (a + 1) * GRID_W)
            tiles = {t: s[t // per_blk][rs, (t % per_blk) * HEAD_DIM:(t % per_blk + 1) * HEAD_DIM]
                        + bias_ref[h, rs, t * HEAD_DIM:(t + 1) * HEAD_DIM]
                     for t in range(n_tiles) if need[a][t]}
            m = functools.reduce(jnp.maximum, tiles.values()).max(axis=-1, keepdims=True)
            p = {t: jnp.exp(v - m) for t, v in tiles.items()}
            l_rows.append(functools.reduce(jnp.add, p.values()).sum(axis=-1, keepdims=True))
            p_rows.append([p[t].astype(BF16) if t in p else zeros for t in range(n_tiles)])
        acc = None
        for i in range(K_BLOCKS):
            p_blk = jnp.concatenate(
                [jnp.concatenate(row[i * per_blk:(i + 1) * per_blk], axis=1) for row in p_rows], axis=0)
            part = jnp.dot(p_blk, v_refs[i][:, hs], preferred_element_type=F32)
            acc = part if acc is None else acc + part
        o_ref[:, hs] = (acc * (1.0 / jnp.concatenate(l_rows, axis=0))).astype(o_ref.dtype)


def _attn_kernel(q_ref, k0_ref, k1_ref, k2_ref, v0_ref, v1_ref, v2_ref, bias_ref, o_ref, *,
                 n_heads, need):
    j = pl.program_id(0)
    last = pl.num_programs(0) - 1
    block_type = jnp.where(j == 0, 0, jnp.where(j == last, 2, 1))
    for t in range(3):
        @pl.when(block_type == t)
        def _():
            _attn_heads(q_ref, (k0_ref, k1_ref, k2_ref), (v0_ref, v1_ref, v2_ref), bias_ref, o_ref,
                        n_heads=n_heads, need=need[t])


def _attention(u, bias, *, batch, seq, d_attn):
    t = u.shape[0]
    row_ok, _ = _attn_window(seq // GRID_W)
    pairs = row_ok.reshape(3, Q_ROWS, -1, HEAD_DIM // GRID_W)
    need = tuple(tuple(tuple(bool(x) for x in row) for row in typ) for typ in pairs.any(axis=-1))
    n_heads = d_attn // HEAD_DIM
    nblk = seq // Q_TOK
    assert nblk >= K_BLOCKS

    def q_map(j, b):
        return (b * nblk + j, 0)

    def kv_map(i, col):
        def f(j, b):
            return (b * nblk + jnp.clip(j - 1, 0, nblk - K_BLOCKS) + i, col)
        return f

    def bias_map(j, b):
        return (jnp.where(j == 0, 0, jnp.where(j == nblk - 1, 2, 1)), 0, 0, 0)

    blk = pl.BlockSpec((Q_TOK, d_attn), q_map)
    in_specs = [blk]
    in_specs += [pl.BlockSpec((Q_TOK, d_attn), kv_map(i, 1)) for i in range(K_BLOCKS)]
    in_specs += [pl.BlockSpec((Q_TOK, d_attn), kv_map(i, 2)) for i in range(K_BLOCKS)]
    in_specs += [pl.BlockSpec((None, n_heads, Q_TOK, K_BLOCKS * Q_TOK), bias_map)]
    est = 2 * 8 * Q_TOK * d_attn * 2 + 2 * n_heads * Q_TOK * K_BLOCKS * Q_TOK * 4
    return pl.pallas_call(
        functools.partial(_attn_kernel, n_heads=n_heads, need=need),
        out_shape=jax.ShapeDtypeStruct((t, d_attn), BF16),
        grid=(nblk, batch),
        in_specs=in_specs,
        out_specs=blk,
        compiler_params=_params(("arbitrary", "arbitrary"), est),
        name="nattn",
    )(u, u, u, u, u, u, u, bias)


def _scan_pitch(n_chunks):
    groups = n_chunks // V7X_SUBLANES
    return (groups + 1 - groups % 2) * V7X_SUBLANES


def _sigmoid(x):
    return 0.5 * jnp.tanh(0.5 * x) + 0.5


def _rglru_kernel(xr_ref, yg_ref, cw_ref, cb_ref, wg_ref, bg_ref, lam_ref, o_ref,
                  xp_ref, a_ref, b_ref, h_ref, cr_ref, *, seq):
    n_chunks = seq // SCAN_CHUNK
    pitch = _scan_pitch(n_chunks)
    w = REC_BLOCK_W
    halo = V7X_BF16_ROWS

    zeros = jnp.zeros((halo, w), xp_ref.dtype)
    xp_ref[0:halo, :] = zeros
    xp_ref[seq + halo:seq + 2 * halo, :] = zeros

    def stage(c, carry):
        r = pl.multiple_of(c * SCAN_CHUNK, SCAN_CHUNK)
        xp_ref[pl.ds(r + halo, SCAN_CHUNK), :] = xr_ref[pl.ds(r, SCAN_CHUNK), :].astype(F32)
        return carry
    lax.fori_loop(0, n_chunks, stage, 0)

    z = -lam_ref[...]
    half_coef = (-0.5 * C_RG * LOG2_E) * (jnp.maximum(z, 0.0) + jnp.log1p(jnp.exp(-jnp.abs(z))))

    left = CONV_W // 2

    def gates(c, carry):
        r = pl.multiple_of(c * SCAN_CHUNK, SCAN_CHUNK)
        xc = cb_ref[...]
        for j in range(CONV_W):
            xc = xc + xp_ref[pl.ds(r + (halo + j - left), SCAN_CHUNK), :] * cw_ref[j:j + 1, :]
        th = jnp.tanh(jnp.dot(xc.astype(BF16), wg_ref[...], preferred_element_type=F32) + bg_ref[...])
        xh = 0.5 * xc
        chunk_rows = pl.ds(c, SCAN_CHUNK, stride=pitch)
        for d in range(2):
            a = jnp.exp2(th[:, d * w:(d + 1) * w] * half_coef[d:d + 1, :] + half_coef[d:d + 1, :])
            y = 1.0 - a * a
            root = jnp.where(y > 0.0, y * lax.rsqrt(y), 0.0)
            a_ref[d, chunk_rows, :] = a
            b_ref[d, chunk_rows, :] = root * (th[:, (2 + d) * w:(3 + d) * w] + 1.0) * xh
        return carry
    lax.fori_loop(0, n_chunks, gates, 0, unroll=GATE_UNROLL)

    def step_rows(d, t):
        pos = t if d == 0 else SCAN_CHUNK - 1 - t
        return pl.ds(pl.multiple_of(pos * pitch, V7X_SUBLANES), n_chunks)

    def pass1(t, carry):
        hs, ps = carry
        a = [a_ref[d, step_rows(d, t), :] for d in range(2)]
        return (tuple(a[d] * hs[d] + b_ref[d, step_rows(d, t), :] for d in range(2)),
                tuple(a[d] * ps[d] for d in range(2)))

    zero = jnp.zeros((n_chunks, w), F32)
    he, pe = lax.fori_loop(0, SCAN_CHUNK, pass1, ((zero, zero), (zero + 1.0, zero + 1.0)),
                           unroll=SCAN_UNROLL)

    for d in range(2):
        order = range(n_chunks) if d == 0 else range(n_chunks - 1, -1, -1)
        carry = jnp.zeros((1, w), F32)
        for j in order:
            cr_ref[d, j:j + 1, :] = carry
            carry = he[d][j:j + 1, :] + pe[d][j:j + 1, :] * carry

    def pass2(t, hs):
        nh = []
        for d in range(2):
            h = a_ref[d, step_rows(d, t), :] * hs[d] + b_ref[d, step_rows(d, t), :]
            h_ref[d, step_rows(d, t), :] = h
            nh.append(h)
        return tuple(nh)

    lax.fori_loop(0, SCAN_CHUNK, pass2, (cr_ref[0], cr_ref[1]), unroll=SCAN_UNROLL)

    def combine(c, carry):
        r = pl.multiple_of(c * SCAN_CHUNK, SCAN_CHUNK)
        chunk_rows = pl.ds(c, SCAN_CHUNK, stride=pitch)
        h = h_ref[0, chunk_rows, :] + h_ref[1, chunk_rows, :]
        y = yg_ref[pl.ds(r, SCAN_CHUNK), :].astype(F32)
        o_ref[pl.ds(r, SCAN_CHUNK), :] = (h * jax.nn.gelu(y)).astype(o_ref.dtype)
        return carry
    lax.fori_loop(0, n_chunks, combine, 0)


def _rglru(slabs, conv_w, conv_b, wg, bg, lam, *, d_rec):
    _, b, s, w = slabs.shape
    nb = d_rec // w
    assert s % (SCAN_CHUNK * V7X_SUBLANES) == 0
    n_chunks = s // SCAN_CHUNK
    scan_buf = pltpu.VMEM((2, SCAN_CHUNK * _scan_pitch(n_chunks), w), F32)
    est = (3 * 2 * s * w * 2 + (s + 2 * V7X_BF16_ROWS) * w * 4
           + 3 * 2 * SCAN_CHUNK * _scan_pitch(n_chunks) * w * 4)
    return pl.pallas_call(
        functools.partial(_rglru_kernel, seq=s),
        out_shape=jax.ShapeDtypeStruct((b, s, d_rec), BF16),
        grid=(b, nb),
        in_specs=[pl.BlockSpec((None, None, s, w), lambda i, n: (n, i, 0, 0)),
                  pl.BlockSpec((None, None, s, w), lambda i, n: (nb + n, i, 0, 0)),
                  pl.BlockSpec((CONV_W, w), lambda i, n: (0, n)),
                  pl.BlockSpec((1, w), lambda i, n: (0, n)),
                  pl.BlockSpec((None, w, 4 * w), lambda i, n: (n, 0, 0)),
                  pl.BlockSpec((None, 1, 4 * w), lambda i, n: (n, 0, 0)),
                  pl.BlockSpec((2, w), lambda i, n: (0, n))],
        out_specs=pl.BlockSpec((None, s, w), lambda i, n: (i, 0, n)),
        scratch_shapes=[pltpu.VMEM((s + 2 * V7X_BF16_ROWS, w), F32),
                        scan_buf, scan_buf, scan_buf, pltpu.VMEM((2, n_chunks, w), F32)],
        compiler_params=_params(("parallel", "arbitrary"), est),
        name="rglru",
    )(slabs, slabs, conv_w, conv_b, wg, bg, lam)


def _outproj_kernel(a_ref, r_ref, x_ref, ga_ref, gr_ref, w_ref, gp_ref, o_ref, cat_ref):
    da = a_ref.shape[1]
    cat_ref[:, :da] = _rms(a_ref[...].astype(F32), ga_ref[...]).astype(BF16)
    cat_ref[:, da:] = _rms(r_ref[...].astype(F32), gr_ref[...]).astype(BF16)
    mixed = jnp.dot(cat_ref[...], w_ref[...], preferred_element_type=F32)
    o_ref[...] = x_ref[...] + _rms(mixed, gp_ref[...])


def _out_proj(attn, rec, x2, g_attn, g_rec, w, g_post, *, tm):
    t, d = x2.shape
    da, dr = attn.shape[1], rec.shape[1]
    est = 2 * tm * (da + dr) * 2 + 4 * tm * d * 4 + (da + dr) * d * 2 + tm * (da + dr) * 2
    row = lambda i: (i, 0)
    fixed = lambda i: (0, 0)
    return pl.pallas_call(
        _outproj_kernel,
        out_shape=jax.ShapeDtypeStruct((t, d), F32),
        grid=(t // tm,),
        in_specs=[pl.BlockSpec((tm, da), row), pl.BlockSpec((tm, dr), row), pl.BlockSpec((tm, d), row),
                  pl.BlockSpec((1, da), fixed), pl.BlockSpec((1, dr), fixed),
                  pl.BlockSpec((da + dr, d), fixed, pipeline_mode=pl.Buffered(1)),
                  pl.BlockSpec((1, d), fixed)],
        out_specs=pl.BlockSpec((tm, d), row),
        scratch_shapes=[pltpu.VMEM((tm, da + dr), BF16)],
        compiler_params=_params(("parallel",), est),
        name="out_proj",
    )(attn, rec, x2, g_attn, g_rec, w, g_post)


def _ffn_kernel(h_ref, gpre_ref, wg_ref, wu_ref, wd_ref, gpost_ref, o_ref, fn_ref, acc_ref, *, row_chunk):
    f = pl.program_id(1)

    @pl.when(f == 0)
    def _():
        def body(c, carry):
            r = pl.multiple_of(c * row_chunk, row_chunk)
            fn_ref[pl.ds(r, row_chunk), :] = _rms(h_ref[pl.ds(r, row_chunk), :], gpre_ref[...]).astype(BF16)
            return carry
        lax.fori_loop(0, h_ref.shape[0] // row_chunk, body, 0, unroll=2)

    fn = fn_ref[...]
    gate = jnp.dot(fn, wg_ref[...], preferred_element_type=F32)
    up = jnp.dot(fn, wu_ref[...], preferred_element_type=F32)
    act = (jax.nn.silu(gate) * up).astype(BF16)
    @pl.when(f == 0)
    def _():
        acc_ref[...] = jnp.dot(act, wd_ref[...], preferred_element_type=F32)

    @pl.when(f > 0)
    def _():
        acc_ref[...] += jnp.dot(act, wd_ref[...], preferred_element_type=F32)

    @pl.when(f == pl.num_programs(1) - 1)
    def _():
        def body(c, carry):
            r = pl.multiple_of(c * row_chunk, row_chunk)
            o_ref[pl.ds(r, row_chunk), :] = _rms(acc_ref[pl.ds(r, row_chunk), :], gpost_ref[...]).astype(o_ref.dtype)
            return carry
        lax.fori_loop(0, h_ref.shape[0] // row_chunk, body, 0, unroll=2)


def _ffn(h1, g_pre, w_gate, w_up, w_down, g_post, *, tm, tf):
    t, d = h1.shape
    dff = w_gate.shape[1]
    est = 2 * tm * d * 4 + 3 * 2 * d * tf * 2 + 2 * tm * d * 2 + tm * d * 2 + tm * d * 4
    return pl.pallas_call(
        functools.partial(_ffn_kernel, row_chunk=min(128, tm)),
        out_shape=jax.ShapeDtypeStruct((t, d), BF16),
        grid=(t // tm, dff // tf),
        in_specs=[pl.BlockSpec((tm, d), lambda i, f: (i, 0)),
                  pl.BlockSpec((1, d), lambda i, f: (0, 0)),
                  pl.BlockSpec((d, tf), lambda i, f: (0, f)),
                  pl.BlockSpec((d, tf), lambda i, f: (0, f)),
                  pl.BlockSpec((tf, d), lambda i, f: (f, 0)),
                  pl.BlockSpec((1, d), lambda i, f: (0, 0))],
        out_specs=pl.BlockSpec((tm, d), lambda i, f: (i, 0)),
        scratch_shapes=[pltpu.VMEM((tm, d), BF16), pltpu.VMEM((tm, d), F32)],
        compiler_params=_params(("parallel", "arbitrary"), est),
        name="ffn",
    )(h1, g_pre, w_gate, w_up, w_down, g_post)


def _ple_kernel(h_ref, d_ref, p_ref, gpre_ref, wg_ref, wp_ref, gpost_ref, o_ref):
    h2 = h_ref[...] + d_ref[...].astype(F32)
    gate = _sigmoid(jnp.dot(_rms(h2, gpre_ref[...]).astype(BF16), wg_ref[...],
                            preferred_element_type=F32))
    ple = jnp.dot(p_ref[...].astype(BF16), wp_ref[...], preferred_element_type=F32)
    o_ref[...] = h2 + _rms(gate * ple, gpost_ref[...])


def _ple(h1, dff, p2, g_pre, w_gate, w_proj, g_post, *, tm):
    t, d = h1.shape
    dp = p2.shape[1]
    est = 4 * tm * d * 4 + 2 * tm * d * 2 + 2 * tm * dp * 4 + d * d * 2 + 2 * dp * d * 2 + 3 * tm * d * 4
    row = lambda i: (i, 0)
    fixed = lambda i: (0, 0)
    return pl.pallas_call(
        _ple_kernel,
        out_shape=jax.ShapeDtypeStruct((t, d), F32),
        grid=(t // tm,),
        in_specs=[pl.BlockSpec((tm, d), row), pl.BlockSpec((tm, d), row), pl.BlockSpec((tm, dp), row),
                  pl.BlockSpec((1, d), fixed),
                  pl.BlockSpec((d, d), fixed, pipeline_mode=pl.Buffered(1)),
                  pl.BlockSpec((dp, d), fixed),
                  pl.BlockSpec((1, d), fixed)],
        out_specs=pl.BlockSpec((tm, d), row),
        compiler_params=_params(("parallel",), est),
        name="ple",
    )(h1, dff, p2, g_pre, w_gate, w_proj, g_post)


def _layer(h, p_i, g_mix_pre, w_in, rpb, conv_w, conv_b, w_rg_a, b_rg_a, w_rg_i, b_rg_i, lam,
           g_attn_out, g_rec_out, w_out, g_mix_post, g_ffn_pre, w_ffn_gate, w_ffn_up, w_ffn_down,
           g_ffn_post, g_ple_pre, w_ple_gate, w_ple_proj, g_ple_post):
    batch, seq, d = h.shape
    t = batch * seq
    d_attn = g_attn_out.shape[0]
    d_rec = g_rec_out.shape[0]
    nb = d_rec // REC_BLOCK_W
    row = lambda v: v.reshape(1, -1).astype(F32)

    x2 = h.reshape(t, d)
    u, u_rec, (w_out_b, w_gate_b, w_up_b, w_down_b, w_pgate_b) = _in_proj(
        x2, row(g_mix_pre), w_in.astype(BF16), [w_out, w_ffn_gate, w_ffn_up, w_ffn_down, w_ple_gate],
        n_main=3 * d_attn, tm=min(512, t), tn=1024)

    attn = _attention(u, _attn_bias(rpb, seq // GRID_W), batch=batch, seq=seq, d_attn=d_attn)

    wg = (0.5 * jnp.concatenate([w_rg_a[0], w_rg_a[1], w_rg_i[0], w_rg_i[1]], axis=-1)).astype(BF16)
    bg = jnp.concatenate([b_rg_a.reshape(2, nb, 1, REC_BLOCK_W)[0], b_rg_a.reshape(2, nb, 1, REC_BLOCK_W)[1],
                          b_rg_i.reshape(2, nb, 1, REC_BLOCK_W)[0], b_rg_i.reshape(2, nb, 1, REC_BLOCK_W)[1]],
                         axis=-1).astype(F32) * 0.5
    rec = _rglru(u_rec.reshape(-1, batch, seq, REC_BLOCK_W), conv_w.astype(F32), row(conv_b), wg, bg,
                 lam.astype(F32), d_rec=d_rec)

    h1 = _out_proj(attn, rec.reshape(t, d_rec), x2, row(g_attn_out), row(g_rec_out),
                   w_out_b, row(g_mix_post), tm=min(512, t))

    dff = _ffn(h1, row(g_ffn_pre), w_gate_b, w_up_b,
               w_down_b, row(g_ffn_post), tm=min(1024, t), tf=512)

    out = _ple(h1, dff, p_i.reshape(t, -1), row(g_ple_pre), w_pgate_b,
               w_ple_proj.astype(BF16), row(g_ple_post), tm=min(512, t))
    return out.reshape(batch, seq, d)


def kernel(x, p, g_mix_pre, w_in, rpb, conv_w, conv_b, w_rg_a, b_rg_a, w_rg_i, b_rg_i, lam, g_attn_out, g_rec_out, w_out, g_mix_post, g_ffn_pre, w_ffn_gate, w_ffn_up, w_ffn_down, g_ffn_post, g_ple_pre, w_ple_gate, w_ple_proj, g_ple_post):
    h = x
    for i in range(p.shape[0]):
        h = _layer(h, p[i], g_mix_pre[i], w_in[i], rpb[i], conv_w[i], conv_b[i], w_rg_a[i], b_rg_a[i],
                   w_rg_i[i], b_rg_i[i], lam[i], g_attn_out[i], g_rec_out[i], w_out[i], g_mix_post[i],
                   g_ffn_pre[i], w_ffn_gate[i], w_ffn_up[i], w_ffn_down[i], g_ffn_post[i], g_ple_pre[i],
                   w_ple_gate[i], w_ple_proj[i], g_ple_post[i])
    return h
```

```python
import functools

import numpy as np
import jax
import jax.numpy as jnp
from jax import lax
from jax.experimental import pallas as pl
from jax.experimental.pallas import tpu as pltpu

F32 = jnp.float32
BF16 = jnp.bfloat16

EPS = 1e-6
NEG_INF = -1e9
LOG2_E = 1.4426950408889634
GRID_W = 64
HEAD_DIM = 128
WIN_R = 8
WIN_C = 16
CONV_W = 4
C_RG = 8.0
REC_BLOCK_W = 128

V7X_SUBLANES = 8
V7X_BF16_ROWS = 16
V7X_VMEM_BYTES = 64 * 1024 * 1024
V7X_VMEM_CAP = V7X_VMEM_BYTES - 8 * 1024 * 1024

Q_ROWS = 4
Q_TOK = Q_ROWS * GRID_W
K_BLOCKS = 3

SCAN_CHUNK = 256
GATE_UNROLL = 4
SCAN_UNROLL = 8


def _vmem_limit(estimate_bytes):
    return int(min(estimate_bytes * 5 // 4 + (4 << 20), V7X_VMEM_CAP))


def _params(semantics, estimate_bytes):
    return pltpu.CompilerParams(dimension_semantics=semantics,
                                vmem_limit_bytes=_vmem_limit(estimate_bytes))


def _rms(x, g):
    ms = jnp.mean(x * x, axis=-1, keepdims=True)
    return x * lax.rsqrt(ms + EPS) * g


def _inproj_kernel(*refs, tn, n_riders):
    x_ref, g_ref, w_ref = refs[:3]
    rider_src = refs[3:3 + n_riders]
    o_ref, slab_ref = refs[3 + n_riders:5 + n_riders]
    rider_dst = refs[5 + n_riders:5 + 2 * n_riders]
    hn_ref = refs[5 + 2 * n_riders]
    for src, dst in zip(rider_src, rider_dst):
        dst[...] = src[...].astype(dst.dtype)
    hn_ref[...] = _rms(x_ref[...], g_ref[...]).astype(BF16)
    n_main = o_ref.shape[1]
    w = REC_BLOCK_W
    for c in range(0, w_ref.shape[1], tn):
        res = jnp.dot(hn_ref[...], w_ref[:, c:c + tn], preferred_element_type=F32).astype(o_ref.dtype)
        if c < n_main:
            o_ref[:, c:c + tn] = res
        else:
            for k in range(tn // w):
                slab_ref[(c - n_main) // w + k] = res[:, k * w:(k + 1) * w]


def _rider_block(rows, steps):
    share = 1
    while (rows * share) % (steps * V7X_BF16_ROWS) != 0:
        share *= 2
    return rows * share // steps, share


def _in_proj(x2, g, w, riders, *, n_main, tm, tn):
    t, d = x2.shape
    n = w.shape[1]
    assert n_main % tn == 0 and (n - n_main) % tn == 0 and tn % REC_BLOCK_W == 0
    n_slabs = (n - n_main) // REC_BLOCK_W
    steps = t // tm
    est = 2 * tm * d * 4 + d * n * 2 + 2 * tm * n * 2 + tm * d * 2 + tm * d * 4
    rider_specs = []
    for r in riders:
        rows, share = _rider_block(r.shape[0], steps)
        rider_specs.append(pl.BlockSpec((rows, r.shape[1]), lambda i, share=share: (i // share, 0)))
        est += 2 * rows * r.shape[1] * (4 + 2)
    out = pl.pallas_call(
        functools.partial(_inproj_kernel, tn=tn, n_riders=len(riders)),
        out_shape=[jax.ShapeDtypeStruct((t, n_main), BF16),
                   jax.ShapeDtypeStruct((n_slabs, t, REC_BLOCK_W), BF16)]
                  + [jax.ShapeDtypeStruct(r.shape, BF16) for r in riders],
        grid=(steps,),
        in_specs=[pl.BlockSpec((tm, d), lambda i: (i, 0)),
                  pl.BlockSpec((1, d), lambda i: (0, 0)),
                  pl.BlockSpec((d, n), lambda i: (0, 0), pipeline_mode=pl.Buffered(1))] + rider_specs,
        out_specs=[pl.BlockSpec((tm, n_main), lambda i: (i, 0)),
                   pl.BlockSpec((n_slabs, tm, REC_BLOCK_W), lambda i: (0, i, 0))] + rider_specs,
        scratch_shapes=[pltpu.VMEM((tm, d), BF16)],
        compiler_params=_params(("arbitrary",), est),
        name="in_proj",
    )(x2, g, w, *riders)
    return out[0], out[1], out[2:]


def _attn_window(rows):
    nblk = rows // Q_ROWS
    kr = min(WIN_R, rows)
    n_dr = 2 * WIN_R - 1
    q_row0 = np.array([0, Q_ROWS, rows - Q_ROWS])
    k_row0 = np.array([0, 0, (nblk - K_BLOCKS) * Q_ROWS])
    qr = q_row0[:, None] + np.arange(Q_ROWS)[None, :]
    kro = k_row0[:, None] + np.arange(K_BLOCKS * Q_ROWS)[None, :]
    rstart = np.clip(qr - kr // 2, 0, rows - kr)
    row_ok = (kro[:, None, :] >= rstart[:, :, None]) & (kro[:, None, :] < rstart[:, :, None] + kr)
    dr = np.clip(kro[:, None, :] - qr[:, :, None] + WIN_R - 1, 0, n_dr - 1)
    return row_ok, dr


def _bias_kernel(rpb_ref, o_ref, *, row_ok, dr):
    lanes = 2 * GRID_W
    qc = lax.broadcasted_iota(jnp.int32, (GRID_W, lanes), 0)
    lane = lax.broadcasted_iota(jnp.int32, (GRID_W, lanes), 1)
    kc = lane & (GRID_W - 1)
    cstart = jnp.clip(qc - WIN_C // 2, 0, GRID_W - WIN_C)
    col_ok = (kc >= cstart) & (kc < cstart + WIN_C)
    first = lane < GRID_W
    neg = jnp.full((GRID_W, lanes), NEG_INF, F32)

    toeplitz = {}

    def block(r, second):
        if (r, second) not in toeplitz:
            row = jnp.broadcast_to(rpb_ref[r:r + 1, :], (GRID_W, lanes))
            t = pltpu.roll(row, lanes - (WIN_C - 1), 1, stride=1, stride_axis=0)
            toeplitz[(r, False)] = t
            toeplitz[(r, True)] = pltpu.roll(t, GRID_W, 1)
        return toeplitz[(r, second)]

    for t in range(3):
        for a in range(Q_ROWS):
            for pair in range(K_BLOCKS * Q_ROWS // 2):
                l0, l1 = 2 * pair, 2 * pair + 1
                left = block(int(dr[t, a, l0]), False) if row_ok[t, a, l0] else neg
                right = block(int(dr[t, a, l1]), True) if row_ok[t, a, l1] else neg
                tile = jnp.where(col_ok, jnp.where(first, left, right), neg)
                o_ref[t, a * GRID_W:(a + 1) * GRID_W, pair * lanes:(pair + 1) * lanes] = tile


def _attn_bias(rpb, rows):
    n_heads, n_dr, n_dc = rpb.shape
    row_ok, dr = _attn_window(rows)
    padded = jnp.zeros((n_heads, -(-n_dr // V7X_SUBLANES) * V7X_SUBLANES, 2 * GRID_W), F32)
    padded = padded.at[:, :n_dr, :n_dc].set(rpb.astype(F32))
    shape = (3, n_heads, Q_TOK, K_BLOCKS * Q_TOK)
    return pl.pallas_call(
        functools.partial(_bias_kernel, row_ok=row_ok, dr=dr),
        out_shape=jax.ShapeDtypeStruct(shape, F32),
        grid=(n_heads,),
        in_specs=[pl.BlockSpec((None,) + padded.shape[1:], lambda h: (h, 0, 0))],
        out_specs=pl.BlockSpec((3, None) + shape[2:], lambda h: (0, h, 0, 0)),
        compiler_params=_params(("parallel",), 2 * 3 * shape[2] * shape[3] * 4),
        name="attn_bias",
    )(padded)


def _attn_heads(q_ref, k_refs, v_refs, bias_ref, o_ref, *, n_heads, need):
    scale = HEAD_DIM ** -0.5
    n_tiles = len(need[0])
    per_blk = n_tiles // K_BLOCKS
    zeros = jnp.zeros((GRID_W, HEAD_DIM), BF16)
    for h in range(n_heads):
        hs = slice(h * HEAD_DIM, (h + 1) * HEAD_DIM)
        q = (q_ref[:, hs].astype(F32) * scale).astype(BF16)
        s = [lax.dot_general(q, k_refs[i][:, hs], (((1,), (1,)), ((), ())),
                             preferred_element_type=F32) for i in range(K_BLOCKS)]
        p_rows, l_rows = [], []
        for a in range(Q_ROWS):
            rs = slice(a * GRID_W, (a + 1) * GRID_W)
            tiles = {t: s[t // per_blk][rs, (t % per_blk) * HEAD_DIM:(t % per_blk + 1) * HEAD_DIM]
                        + bias_ref[h, rs, t * HEAD_DIM:(t + 1) * HEAD_DIM]
                     for t in range(n_tiles) if need[a][t]}
            m = functools.reduce(jnp.maximum, tiles.values()).max(axis=-1, keepdims=True)
            p = {t: jnp.exp(v - m) for t, v in tiles.items()}
            l_rows.append(functools.reduce(jnp.add, p.values()).sum(axis=-1, keepdims=True))
            p_rows.append([p[t].astype(BF16) if t in p else zeros for t in range(n_tiles)])
        acc = None
        for i in range(K_BLOCKS):
            p_blk = jnp.concatenate(
                [jnp.concatenate(row[i * per_blk:(i + 1) * per_blk], axis=1) for row in p_rows], axis=0)
            part = jnp.dot(p_blk, v_refs[i][:, hs], preferred_element_type=F32)
            acc = part if acc is None else acc + part
        o_ref[:, hs] = (acc * (1.0 / jnp.concatenate(l_rows, axis=0))).astype(o_ref.dtype)


def _attn_kernel(q_ref, k0_ref, k1_ref, k2_ref, v0_ref, v1_ref, v2_ref, bias_ref, o_ref, *,
                 n_heads, need):
    j = pl.program_id(0)
    last = pl.num_programs(0) - 1
    block_type = jnp.where(j == 0, 0, jnp.where(j == last, 2, 1))
    for t in range(3):
        @pl.when(block_type == t)
        def _():
            _attn_heads(q_ref, (k0_ref, k1_ref, k2_ref), (v0_ref, v1_ref, v2_ref), bias_ref, o_ref,
                        n_heads=n_heads, need=need[t])


def _attention(u, bias, *, batch, seq, d_attn):
    t = u.shape[0]
    row_ok, _ = _attn_window(seq // GRID_W)
    pairs = row_ok.reshape(3, Q_ROWS, -1, HEAD_DIM // GRID_W)
    need = tuple(tuple(tuple(bool(x) for x in row) for row in typ) for typ in pairs.any(axis=-1))
    n_heads = d_attn // HEAD_DIM
    nblk = seq // Q_TOK
    assert nblk >= K_BLOCKS

    def q_map(j, b):
        return (b * nblk + j, 0)

    def kv_map(i, col):
        def f(j, b):
            return (b * nblk + jnp.clip(j - 1, 0, nblk - K_BLOCKS) + i, col)
        return f

    def bias_map(j, b):
        return (jnp.where(j == 0, 0, jnp.where(j == nblk - 1, 2, 1)), 0, 0, 0)

    blk = pl.BlockSpec((Q_TOK, d_attn), q_map)
    in_specs = [blk]
    in_specs += [pl.BlockSpec((Q_TOK, d_attn), kv_map(i, 1)) for i in range(K_BLOCKS)]
    in_specs += [pl.BlockSpec((Q_TOK, d_attn), kv_map(i, 2)) for i in range(K_BLOCKS)]
    in_specs += [pl.BlockSpec((None, n_heads, Q_TOK, K_BLOCKS * Q_TOK), bias_map)]
    est = 2 * 8 * Q_TOK * d_attn * 2 + 2 * n_heads * Q_TOK * K_BLOCKS * Q_TOK * 4
    return pl.pallas_call(
        functools.partial(_attn_kernel, n_heads=n_heads, need=need),
        out_shape=jax.ShapeDtypeStruct((t, d_attn), BF16),
        grid=(nblk, batch),
        in_specs=in_specs,
        out_specs=blk,
        compiler_params=_params(("arbitrary", "arbitrary"), est),
        name="nattn",
    )(u, u, u, u, u, u, u, bias)


def _scan_pitch(n_chunks):
    return n_chunks + 4 if (n_chunks // 4) % 2 == 0 else n_chunks + 8


def _sigmoid(x):
    return 0.5 * jnp.tanh(0.5 * x) + 0.5


def _rglru_kernel(xr_ref, yg_ref, cw_ref, cb_ref, wg_ref, bg_ref, lam_ref, o_ref,
                  xp_ref, a_ref, b_ref, h_ref, cr_ref, *, seq):
    n_chunks = seq // SCAN_CHUNK
    pitch = _scan_pitch(n_chunks)
    w = REC_BLOCK_W
    halo = V7X_BF16_ROWS

    zeros = jnp.zeros((halo, w), xp_ref.dtype)
    xp_ref[0:halo, :] = zeros
    xp_ref[seq + halo:seq + 2 * halo, :] = zeros

    def stage(c, carry):
        r = pl.multiple_of(c * SCAN_CHUNK, SCAN_CHUNK)
        xp_ref[pl.ds(r + halo, SCAN_CHUNK), :] = xr_ref[pl.ds(r, SCAN_CHUNK), :].astype(F32)
        return carry
    lax.fori_loop(0, n_chunks, stage, 0)

    z = -lam_ref[...]
    half_coef = (-0.5 * C_RG * LOG2_E) * (jnp.maximum(z, 0.0) + jnp.log1p(jnp.exp(-jnp.abs(z))))

    left = CONV_W // 2

    def gates(c, carry):
        r = pl.multiple_of(c * SCAN_CHUNK, SCAN_CHUNK)
        xc = cb_ref[...]
        for j in range(CONV_W):
            xc = xc + xp_ref[pl.ds(r + (halo + j - left), SCAN_CHUNK), :] * cw_ref[j:j + 1, :]
        th = jnp.tanh(jnp.dot(xc.astype(BF16), wg_ref[...], preferred_element_type=F32) + bg_ref[...])
        xh = 0.5 * xc
        chunk_rows = pl.ds(c, SCAN_CHUNK, stride=pitch)
        for d in range(2):
            a = jnp.exp2(th[:, d * w:(d + 1) * w] * half_coef[d:d + 1, :] + half_coef[d:d + 1, :])
            y = 1.0 - a * a
            root = jnp.where(y > 0.0, y * lax.rsqrt(y), 0.0)
            a_ref[d, chunk_rows, :] = a
            b_ref[d, chunk_rows, :] = root * (th[:, (2 + d) * w:(3 + d) * w] + 1.0) * xh
        return carry
    lax.fori_loop(0, n_chunks, gates, 0, unroll=GATE_UNROLL)

    def step_rows(d, t):
        pos = t if d == 0 else SCAN_CHUNK - 1 - t
        return pl.ds(pos * pitch, n_chunks)

    def pass1(t, carry):
        hs, ps = carry
        a = [a_ref[d, step_rows(d, t), :] for d in range(2)]
        return (tuple(a[d] * hs[d] + b_ref[d, step_rows(d, t), :] for d in range(2)),
                tuple(a[d] * ps[d] for d in range(2)))

    zero = jnp.zeros((n_chunks, w), F32)
    he, pe = lax.fori_loop(0, SCAN_CHUNK, pass1, ((zero, zero), (zero + 1.0, zero + 1.0)),
                           unroll=SCAN_UNROLL)

    for d in range(2):
        order = range(n_chunks) if d == 0 else range(n_chunks - 1, -1, -1)
        carry = jnp.zeros((1, w), F32)
        for j in order:
            cr_ref[d, j:j + 1, :] = carry
            carry = he[d][j:j + 1, :] + pe[d][j:j + 1, :] * carry

    def pass2(t, hs):
        nh = []
        for d in range(2):
            h = a_ref[d, step_rows(d, t), :] * hs[d] + b_ref[d, step_rows(d, t), :]
            h_ref[d, step_rows(d, t), :] = h
            nh.append(h)
        return tuple(nh)

    lax.fori_loop(0, SCAN_CHUNK, pass2, (cr_ref[0], cr_ref[1]), unroll=SCAN_UNROLL)

    def combine(c, carry):
        r = pl.multiple_of(c * SCAN_CHUNK, SCAN_CHUNK)
        chunk_rows = pl.ds(c, SCAN_CHUNK, stride=pitch)
        h = h_ref[0, chunk_rows, :] + h_ref[1, chunk_rows, :]
        y = yg_ref[pl.ds(r, SCAN_CHUNK), :].astype(F32)
        o_ref[pl.ds(r, SCAN_CHUNK), :] = (h * jax.nn.gelu(y)).astype(o_ref.dtype)
        return carry
    lax.fori_loop(0, n_chunks, combine, 0)


def _rglru(slabs, conv_w, conv_b, wg, bg, lam, *, d_rec):
    _, b, s, w = slabs.shape
    nb = d_rec // w
    assert s % (SCAN_CHUNK * V7X_SUBLANES) == 0
    n_chunks = s // SCAN_CHUNK
    scan_buf = pltpu.VMEM((2, SCAN_CHUNK * _scan_pitch(n_chunks), w), F32)
    est = (3 * 2 * s * w * 2 + (s + 2 * V7X_BF16_ROWS) * w * 4
           + 3 * 2 * SCAN_CHUNK * _scan_pitch(n_chunks) * w * 4)
    return pl.pallas_call(
        functools.partial(_rglru_kernel, seq=s),
        out_shape=jax.ShapeDtypeStruct((b, s, d_rec), BF16),
        grid=(b, nb),
        in_specs=[pl.BlockSpec((None, None, s, w), lambda i, n: (n, i, 0, 0)),
                  pl.BlockSpec((None, None, s, w), lambda i, n: (nb + n, i, 0, 0)),
                  pl.BlockSpec((CONV_W, w), lambda i, n: (0, n)),
                  pl.BlockSpec((1, w), lambda i, n: (0, n)),
                  pl.BlockSpec((None, w, 4 * w), lambda i, n: (n, 0, 0)),
                  pl.BlockSpec((None, 1, 4 * w), lambda i, n: (n, 0, 0)),
                  pl.BlockSpec((2, w), lambda i, n: (0, n))],
        out_specs=pl.BlockSpec((None, s, w), lambda i, n: (i, 0, n)),
        scratch_shapes=[pltpu.VMEM((s + 2 * V7X_BF16_ROWS, w), F32),
                        scan_buf, scan_buf, scan_buf, pltpu.VMEM((2, n_chunks, w), F32)],
        compiler_params=_params(("parallel", "arbitrary"), est),
        name="rglru",
    )(slabs, slabs, conv_w, conv_b, wg, bg, lam)


def _outproj_kernel(a_ref, r_ref, x_ref, ga_ref, gr_ref, w_ref, gp_ref, o_ref, cat_ref):
    da = a_ref.shape[1]
    cat_ref[:, :da] = _rms(a_ref[...].astype(F32), ga_ref[...]).astype(BF16)
    cat_ref[:, da:] = _rms(r_ref[...].astype(F32), gr_ref[...]).astype(BF16)
    mixed = jnp.dot(cat_ref[...], w_ref[...], preferred_element_type=F32)
    o_ref[...] = x_ref[...] + _rms(mixed, gp_ref[...])


def _out_proj(attn, rec, x2, g_attn, g_rec, w, g_post, *, tm):
    t, d = x2.shape
    da, dr = attn.shape[1], rec.shape[1]
    est = 2 * tm * (da + dr) * 2 + 4 * tm * d * 4 + (da + dr) * d * 2 + tm * (da + dr) * 2
    row = lambda i: (i, 0)
    fixed = lambda i: (0, 0)
    return pl.pallas_call(
        _outproj_kernel,
        out_shape=jax.ShapeDtypeStruct((t, d), F32),
        grid=(t // tm,),
        in_specs=[pl.BlockSpec((tm, da), row), pl.BlockSpec((tm, dr), row), pl.BlockSpec((tm, d), row),
                  pl.BlockSpec((1, da), fixed), pl.BlockSpec((1, dr), fixed),
                  pl.BlockSpec((da + dr, d), fixed, pipeline_mode=pl.Buffered(1)),
                  pl.BlockSpec((1, d), fixed)],
        out_specs=pl.BlockSpec((tm, d), row),
        scratch_shapes=[pltpu.VMEM((tm, da + dr), BF16)],
        compiler_params=_params(("parallel",), est),
        name="out_proj",
    )(attn, rec, x2, g_attn, g_rec, w, g_post)


def _ffn_kernel(h_ref, gpre_ref, wg_ref, wu_ref, wd_ref, gpost_ref, o_ref, fn_ref, acc_ref, *, row_chunk):
    f = pl.program_id(1)

    @pl.when(f == 0)
    def _():
        def body(c, carry):
            r = pl.multiple_of(c * row_chunk, row_chunk)
            fn_ref[pl.ds(r, row_chunk), :] = _rms(h_ref[pl.ds(r, row_chunk), :], gpre_ref[...]).astype(BF16)
            return carry
        lax.fori_loop(0, h_ref.shape[0] // row_chunk, body, 0, unroll=2)

    fn = fn_ref[...]
    gate = jnp.dot(fn, wg_ref[...], preferred_element_type=F32)
    up = jnp.dot(fn, wu_ref[...], preferred_element_type=F32)
    act = (jax.nn.silu(gate) * up).astype(BF16)
    @pl.when(f == 0)
    def _():
        acc_ref[...] = jnp.dot(act, wd_ref[...], preferred_element_type=F32)

    @pl.when(f > 0)
    def _():
        acc_ref[...] += jnp.dot(act, wd_ref[...], preferred_element_type=F32)

    @pl.when(f == pl.num_programs(1) - 1)
    def _():
        def body(c, carry):
            r = pl.multiple_of(c * row_chunk, row_chunk)
            o_ref[pl.ds(r, row_chunk), :] = _rms(acc_ref[pl.ds(r, row_chunk), :], gpost_ref[...]).astype(o_ref.dtype)
            return carry
        lax.fori_loop(0, h_ref.shape[0] // row_chunk, body, 0, unroll=2)


def _ffn(h1, g_pre, w_gate, w_up, w_down, g_post, *, tm, tf):
    t, d = h1.shape
    dff = w_gate.shape[1]
    est = 2 * tm * d * 4 + 3 * 2 * d * tf * 2 + 2 * tm * d * 2 + tm * d * 2 + tm * d * 4
    return pl.pallas_call(
        functools.partial(_ffn_kernel, row_chunk=min(128, tm)),
        out_shape=jax.ShapeDtypeStruct((t, d), BF16),
        grid=(t // tm, dff // tf),
        in_specs=[pl.BlockSpec((tm, d), lambda i, f: (i, 0)),
                  pl.BlockSpec((1, d), lambda i, f: (0, 0)),
                  pl.BlockSpec((d, tf), lambda i, f: (0, f)),
                  pl.BlockSpec((d, tf), lambda i, f: (0, f)),
                  pl.BlockSpec((tf, d), lambda i, f: (f, 0)),
                  pl.BlockSpec((1, d), lambda i, f: (0, 0))],
        out_specs=pl.BlockSpec((tm, d), lambda i, f: (i, 0)),
        scratch_shapes=[pltpu.VMEM((tm, d), BF16), pltpu.VMEM((tm, d), F32)],
        compiler_params=_params(("parallel", "arbitrary"), est),
        name="ffn",
    )(h1, g_pre, w_gate, w_up, w_down, g_post)


def _ple_kernel(h_ref, d_ref, p_ref, gpre_ref, wg_ref, wp_ref, gpost_ref, o_ref):
    h2 = h_ref[...] + d_ref[...].astype(F32)
    gate = _sigmoid(jnp.dot(_rms(h2, gpre_ref[...]).astype(BF16), wg_ref[...],
                            preferred_element_type=F32))
    ple = jnp.dot(p_ref[...].astype(BF16), wp_ref[...], preferred_element_type=F32)
    o_ref[...] = h2 + _rms(gate * ple, gpost_ref[...])


def _ple(h1, dff, p2, g_pre, w_gate, w_proj, g_post, *, tm):
    t, d = h1.shape
    dp = p2.shape[1]
    est = 4 * tm * d * 4 + 2 * tm * d * 2 + 2 * tm * dp * 4 + d * d * 2 + 2 * dp * d * 2 + 3 * tm * d * 4
    row = lambda i: (i, 0)
    fixed = lambda i: (0, 0)
    return pl.pallas_call(
        _ple_kernel,
        out_shape=jax.ShapeDtypeStruct((t, d), F32),
        grid=(t // tm,),
        in_specs=[pl.BlockSpec((tm, d), row), pl.BlockSpec((tm, d), row), pl.BlockSpec((tm, dp), row),
                  pl.BlockSpec((1, d), fixed),
                  pl.BlockSpec((d, d), fixed, pipeline_mode=pl.Buffered(1)),
                  pl.BlockSpec((dp, d), fixed),
                  pl.BlockSpec((1, d), fixed)],
        out_specs=pl.BlockSpec((tm, d), row),
        compiler_params=_params(("parallel",), est),
        name="ple",
    )(h1, dff, p2, g_pre, w_gate, w_proj, g_post)


def _layer(h, p_i, g_mix_pre, w_in, rpb, conv_w, conv_b, w_rg_a, b_rg_a, w_rg_i, b_rg_i, lam,
           g_attn_out, g_rec_out, w_out, g_mix_post, g_ffn_pre, w_ffn_gate, w_ffn_up, w_ffn_down,
           g_ffn_post, g_ple_pre, w_ple_gate, w_ple_proj, g_ple_post):
    batch, seq, d = h.shape
    t = batch * seq
    d_attn = g_attn_out.shape[0]
    d_rec = g_rec_out.shape[0]
    nb = d_rec // REC_BLOCK_W
    row = lambda v: v.reshape(1, -1).astype(F32)

    x2 = h.reshape(t, d)
    u, u_rec, (w_out_b, w_gate_b, w_up_b, w_down_b, w_pgate_b) = _in_proj(
        x2, row(g_mix_pre), w_in.astype(BF16), [w_out, w_ffn_gate, w_ffn_up, w_ffn_down, w_ple_gate],
        n_main=3 * d_attn, tm=min(512, t), tn=1024)

    attn = _attention(u, _attn_bias(rpb, seq // GRID_W), batch=batch, seq=seq, d_attn=d_attn)

    wg = (0.5 * jnp.concatenate([w_rg_a[0], w_rg_a[1], w_rg_i[0], w_rg_i[1]], axis=-1)).astype(BF16)
    bg = jnp.concatenate([b_rg_a.reshape(2, nb, 1, REC_BLOCK_W)[0], b_rg_a.reshape(2, nb, 1, REC_BLOCK_W)[1],
                          b_rg_i.reshape(2, nb, 1, REC_BLOCK_W)[0], b_rg_i.reshape(2, nb, 1, REC_BLOCK_W)[1]],
                         axis=-1).astype(F32) * 0.5
    rec = _rglru(u_rec.reshape(-1, batch, seq, REC_BLOCK_W), conv_w.astype(F32), row(conv_b), wg, bg,
                 lam.astype(F32), d_rec=d_rec)

    h1 = _out_proj(attn, rec.reshape(t, d_rec), x2, row(g_attn_out), row(g_rec_out),
                   w_out_b, row(g_mix_post), tm=min(512, t))

    dff = _ffn(h1, row(g_ffn_pre), w_gate_b, w_up_b,
               w_down_b, row(g_ffn_post), tm=min(1024, t), tf=512)

    out = _ple(h1, dff, p_i.reshape(t, -1), row(g_ple_pre), w_pgate_b,
               w_ple_proj.astype(BF16), row(g_ple_post), tm=min(512, t))
    return out.reshape(batch, seq, d)


def kernel(x, p, g_mix_pre, w_in, rpb, conv_w, conv_b, w_rg_a, b_rg_a, w_rg_i, b_rg_i, lam, g_attn_out, g_rec_out, w_out, g_mix_post, g_ffn_pre, w_ffn_gate, w_ffn_up, w_ffn_down, g_ffn_post, g_ple_pre, w_ple_gate, w_ple_proj, g_ple_post):
    h = x
    for i in range(p.shape[0]):
        h = _layer(h, p[i], g_mix_pre[i], w_in[i], rpb[i], conv_w[i], conv_b[i], w_rg_a[i], b_rg_a[i],
                   w_rg_i[i], b_rg_i[i], lam[i], g_attn_out[i], g_rec_out[i], w_out[i], g_mix_post[i],
                   g_ffn_pre[i], w_ffn_gate[i], w_ffn_up[i], w_ffn_down[i], g_ffn_post[i], g_ple_pre[i],
                   w_ple_gate[i], w_ple_proj[i], g_ple_post[i])
    return h
```

```python
import functools

import numpy as np
import jax
import jax.numpy as jnp
from jax import lax
from jax.experimental import pallas as pl
from jax.experimental.pallas import tpu as pltpu

F32 = jnp.float32
BF16 = jnp.bfloat16

EPS = 1e-6
NEG_INF = -1e9
LOG2_E = 1.4426950408889634
GRID_W = 64
HEAD_DIM = 128
WIN_R = 8
WIN_C = 16
CONV_W = 4
C_RG = 8.0
REC_BLOCK_W = 128

V7X_SUBLANES = 8
V7X_MXU_COLS = 256
V7X_BF16_ROWS = 16
V7X_VMEM_BYTES = 64 * 1024 * 1024
V7X_VMEM_CAP = V7X_VMEM_BYTES - 8 * 1024 * 1024

Q_ROWS = 4
Q_TOK = Q_ROWS * GRID_W
K_BLOCKS = 3

SCAN_CHUNK = 256
GATE_UNROLL = 4
SCAN_UNROLL = 8


def _vmem_limit(estimate_bytes):
    return int(min(estimate_bytes * 5 // 4 + (4 << 20), V7X_VMEM_CAP))


def _params(semantics, estimate_bytes):
    return pltpu.CompilerParams(dimension_semantics=semantics,
                                vmem_limit_bytes=_vmem_limit(estimate_bytes))


def _rms(x, g):
    ms = jnp.mean(x * x, axis=-1, keepdims=True)
    return x * lax.rsqrt(ms + EPS) * g


def _inproj_kernel(*refs, tn, n_riders):
    x_ref, g_ref, w_ref = refs[:3]
    rider_src = refs[3:3 + n_riders]
    o_ref, slab_ref = refs[3 + n_riders:5 + n_riders]
    rider_dst = refs[5 + n_riders:5 + 2 * n_riders]
    hn_ref = refs[5 + 2 * n_riders]
    for src, dst in zip(rider_src, rider_dst):
        dst[...] = src[...].astype(dst.dtype)
    hn_ref[...] = _rms(x_ref[...], g_ref[...]).astype(BF16)
    n_main = o_ref.shape[1]
    w = REC_BLOCK_W
    for c in range(0, w_ref.shape[1], tn):
        res = jnp.dot(hn_ref[...], w_ref[:, c:c + tn], preferred_element_type=F32).astype(o_ref.dtype)
        if c < n_main:
            o_ref[:, c:c + tn] = res
        else:
            for k in range(tn // w):
                slab_ref[(c - n_main) // w + k] = res[:, k * w:(k + 1) * w]


def _rider_block(rows, steps):
    share = 1
    while (rows * share) % (steps * V7X_BF16_ROWS) != 0:
        share *= 2
    return rows * share // steps, share


def _in_proj(x2, g, w, riders, *, n_main, tm, tn):
    t, d = x2.shape
    n = w.shape[1]
    assert n_main % tn == 0 and (n - n_main) % tn == 0 and tn % REC_BLOCK_W == 0
    n_slabs = (n - n_main) // REC_BLOCK_W
    steps = t // tm
    est = 2 * tm * d * 4 + d * n * 2 + 2 * tm * n * 2 + tm * d * 2 + tm * d * 4
    rider_specs = []
    for r in riders:
        rows, share = _rider_block(r.shape[0], steps)
        rider_specs.append(pl.BlockSpec((rows, r.shape[1]), lambda i, share=share: (i // share, 0)))
        est += 2 * rows * r.shape[1] * (4 + 2)
    out = pl.pallas_call(
        functools.partial(_inproj_kernel, tn=tn, n_riders=len(riders)),
        out_shape=[jax.ShapeDtypeStruct((t, n_main), BF16),
                   jax.ShapeDtypeStruct((n_slabs, t, REC_BLOCK_W), BF16)]
                  + [jax.ShapeDtypeStruct(r.shape, BF16) for r in riders],
        grid=(steps,),
        in_specs=[pl.BlockSpec((tm, d), lambda i: (i, 0)),
                  pl.BlockSpec((1, d), lambda i: (0, 0)),
                  pl.BlockSpec((d, n), lambda i: (0, 0), pipeline_mode=pl.Buffered(1))] + rider_specs,
        out_specs=[pl.BlockSpec((tm, n_main), lambda i: (i, 0)),
                   pl.BlockSpec((n_slabs, tm, REC_BLOCK_W), lambda i: (0, i, 0))] + rider_specs,
        scratch_shapes=[pltpu.VMEM((tm, d), BF16)],
        compiler_params=_params(("arbitrary",), est),
        name="in_proj",
    )(x2, g, w, *riders)
    return out[0], out[1], out[2:]


def _attn_window(rows):
    nblk = rows // Q_ROWS
    kr = min(WIN_R, rows)
    n_dr = 2 * WIN_R - 1
    q_row0 = np.array([0, Q_ROWS, rows - Q_ROWS])
    k_row0 = np.array([0, 0, (nblk - K_BLOCKS) * Q_ROWS])
    qr = q_row0[:, None] + np.arange(Q_ROWS)[None, :]
    kro = k_row0[:, None] + np.arange(K_BLOCKS * Q_ROWS)[None, :]
    rstart = np.clip(qr - kr // 2, 0, rows - kr)
    row_ok = (kro[:, None, :] >= rstart[:, :, None]) & (kro[:, None, :] < rstart[:, :, None] + kr)
    dr = np.clip(kro[:, None, :] - qr[:, :, None] + WIN_R - 1, 0, n_dr - 1)
    return row_ok, dr


def _bias_kernel(rpb_ref, o_ref, *, row_ok, dr):
    lanes = 2 * GRID_W
    qc = lax.broadcasted_iota(jnp.int32, (GRID_W, lanes), 0)
    lane = lax.broadcasted_iota(jnp.int32, (GRID_W, lanes), 1)
    kc = lane & (GRID_W - 1)
    cstart = jnp.clip(qc - WIN_C // 2, 0, GRID_W - WIN_C)
    col_ok = (kc >= cstart) & (kc < cstart + WIN_C)
    first = lane < GRID_W
    neg = jnp.full((GRID_W, lanes), NEG_INF, F32)

    toeplitz = {}

    def block(r, second):
        if (r, second) not in toeplitz:
            row = jnp.broadcast_to(rpb_ref[r:r + 1, :], (GRID_W, lanes))
            t = pltpu.roll(row, lanes - (WIN_C - 1), 1, stride=1, stride_axis=0)
            toeplitz[(r, False)] = t
            toeplitz[(r, True)] = pltpu.roll(t, GRID_W, 1)
        return toeplitz[(r, second)]

    for t in range(3):
        for a in range(Q_ROWS):
            for pair in range(K_BLOCKS * Q_ROWS // 2):
                l0, l1 = 2 * pair, 2 * pair + 1
                left = block(int(dr[t, a, l0]), False) if row_ok[t, a, l0] else neg
                right = block(int(dr[t, a, l1]), True) if row_ok[t, a, l1] else neg
                tile = jnp.where(col_ok, jnp.where(first, left, right), neg)
                o_ref[t, a * GRID_W:(a + 1) * GRID_W, pair * lanes:(pair + 1) * lanes] = tile


def _attn_bias(rpb, rows):
    n_heads, n_dr, n_dc = rpb.shape
    row_ok, dr = _attn_window(rows)
    padded = jnp.zeros((n_heads, -(-n_dr // V7X_SUBLANES) * V7X_SUBLANES, 2 * GRID_W), F32)
    padded = padded.at[:, :n_dr, :n_dc].set(rpb.astype(F32))
    shape = (3, n_heads, Q_TOK, K_BLOCKS * Q_TOK)
    return pl.pallas_call(
        functools.partial(_bias_kernel, row_ok=row_ok, dr=dr),
        out_shape=jax.ShapeDtypeStruct(shape, F32),
        grid=(n_heads,),
        in_specs=[pl.BlockSpec((None,) + padded.shape[1:], lambda h: (h, 0, 0))],
        out_specs=pl.BlockSpec((3, None) + shape[2:], lambda h: (0, h, 0, 0)),
        compiler_params=_params(("parallel",), 2 * 3 * shape[2] * shape[3] * 4),
        name="attn_bias",
    )(padded)


def _attn_heads(q_ref, k_refs, v_refs, bias_ref, o_ref, *, n_heads, need):
    scale = HEAD_DIM ** -0.5
    n_tiles = len(need[0])
    per_blk = n_tiles // K_BLOCKS
    zeros = jnp.zeros((GRID_W, HEAD_DIM), BF16)
    for h in range(n_heads):
        hs = slice(h * HEAD_DIM, (h + 1) * HEAD_DIM)
        q = (q_ref[:, hs].astype(F32) * scale).astype(BF16)
        s = [lax.dot_general(q, k_refs[i][:, hs], (((1,), (1,)), ((), ())),
                             preferred_element_type=F32) for i in range(K_BLOCKS)]
        p_rows, l_rows = [], []
        for a in range(Q_ROWS):
            rs = slice(a * GRID_W, (a + 1) * GRID_W)
            tiles = {t: s[t // per_blk][rs, (t % per_blk) * HEAD_DIM:(t % per_blk + 1) * HEAD_DIM]
                        + bias_ref[h, rs, t * HEAD_DIM:(t + 1) * HEAD_DIM]
                     for t in range(n_tiles) if need[a][t]}
            m = functools.reduce(jnp.maximum, tiles.values()).max(axis=-1, keepdims=True)
            p = {t: jnp.exp(v - m) for t, v in tiles.items()}
            l_rows.append(functools.reduce(jnp.add, p.values()).sum(axis=-1, keepdims=True))
            p_rows.append([p[t].astype(BF16) if t in p else zeros for t in range(n_tiles)])
        acc = None
        for i in range(K_BLOCKS):
            p_blk = jnp.concatenate(
                [jnp.concatenate(row[i * per_blk:(i + 1) * per_blk], axis=1) for row in p_rows], axis=0)
            part = jnp.dot(p_blk, v_refs[i][:, hs], preferred_element_type=F32)
            acc = part if acc is None else acc + part
        o_ref[:, hs] = (acc * (1.0 / jnp.concatenate(l_rows, axis=0))).astype(o_ref.dtype)


def _attn_kernel(q_ref, k0_ref, k1_ref, k2_ref, v0_ref, v1_ref, v2_ref, bias_ref, o_ref, *,
                 n_heads, need):
    j = pl.program_id(0)
    last = pl.num_programs(0) - 1
    block_type = jnp.where(j == 0, 0, jnp.where(j == last, 2, 1))
    for t in range(3):
        @pl.when(block_type == t)
        def _():
            _attn_heads(q_ref, (k0_ref, k1_ref, k2_ref), (v0_ref, v1_ref, v2_ref), bias_ref, o_ref,
                        n_heads=n_heads, need=need[t])


def _attention(u, bias, *, batch, seq, d_attn):
    t = u.shape[0]
    row_ok, _ = _attn_window(seq // GRID_W)
    pairs = row_ok.reshape(3, Q_ROWS, -1, HEAD_DIM // GRID_W)
    need = tuple(tuple(tuple(bool(x) for x in row) for row in typ) for typ in pairs.any(axis=-1))
    n_heads = d_attn // HEAD_DIM
    nblk = seq // Q_TOK
    assert nblk >= K_BLOCKS

    def q_map(j, b):
        return (b * nblk + j, 0)

    def kv_map(i, col):
        def f(j, b):
            return (b * nblk + jnp.clip(j - 1, 0, nblk - K_BLOCKS) + i, col)
        return f

    def bias_map(j, b):
        return (jnp.where(j == 0, 0, jnp.where(j == nblk - 1, 2, 1)), 0, 0, 0)

    blk = pl.BlockSpec((Q_TOK, d_attn), q_map)
    in_specs = [blk]
    in_specs += [pl.BlockSpec((Q_TOK, d_attn), kv_map(i, 1)) for i in range(K_BLOCKS)]
    in_specs += [pl.BlockSpec((Q_TOK, d_attn), kv_map(i, 2)) for i in range(K_BLOCKS)]
    in_specs += [pl.BlockSpec((None, n_heads, Q_TOK, K_BLOCKS * Q_TOK), bias_map)]
    est = 2 * 8 * Q_TOK * d_attn * 2 + 2 * n_heads * Q_TOK * K_BLOCKS * Q_TOK * 4
    return pl.pallas_call(
        functools.partial(_attn_kernel, n_heads=n_heads, need=need),
        out_shape=jax.ShapeDtypeStruct((t, d_attn), BF16),
        grid=(nblk, batch),
        in_specs=in_specs,
        out_specs=blk,
        compiler_params=_params(("arbitrary", "arbitrary"), est),
        name="nattn",
    )(u, u, u, u, u, u, u, bias)


def _scan_pitch(n_chunks):
    return n_chunks + 4 if (n_chunks // 4) % 2 == 0 else n_chunks + 8


def _sigmoid(x):
    return 0.5 * jnp.tanh(0.5 * x) + 0.5


def _rglru_kernel(xr_ref, yg_ref, cw_ref, cb_ref, wg_ref, bg_ref, lam_ref, o_ref,
                  xp_ref, a_ref, b_ref, h_ref, cr_ref, *, seq):
    n_chunks = seq // SCAN_CHUNK
    pitch = _scan_pitch(n_chunks)
    w = REC_BLOCK_W
    halo = V7X_BF16_ROWS

    zeros = jnp.zeros((halo, w), xp_ref.dtype)
    xp_ref[0:halo, :] = zeros
    xp_ref[seq + halo:seq + 2 * halo, :] = zeros

    def stage(c, carry):
        r = pl.multiple_of(c * SCAN_CHUNK, SCAN_CHUNK)
        xp_ref[pl.ds(r + halo, SCAN_CHUNK), :] = xr_ref[pl.ds(r, SCAN_CHUNK), :].astype(F32)
        return carry
    lax.fori_loop(0, n_chunks, stage, 0)

    z = -lam_ref[...]
    half_coef = (-0.5 * C_RG * LOG2_E) * (jnp.maximum(z, 0.0) + jnp.log1p(jnp.exp(-jnp.abs(z))))

    left = CONV_W // 2

    def gates(c, carry):
        r = pl.multiple_of(c * SCAN_CHUNK, SCAN_CHUNK)
        xc = cb_ref[...]
        for j in range(CONV_W):
            xc = xc + xp_ref[pl.ds(r + (halo + j - left), SCAN_CHUNK), :] * cw_ref[j:j + 1, :]
        th = jnp.tanh(jnp.dot(xc.astype(BF16), wg_ref[...], preferred_element_type=F32) + bg_ref[...])
        xh = 0.5 * xc
        chunk_rows = pl.ds(c, SCAN_CHUNK, stride=pitch)
        for d in range(2):
            a = jnp.exp2(th[:, d * w:(d + 1) * w] * half_coef[d:d + 1, :] + half_coef[d:d + 1, :])
            y = 1.0 - a * a
            root = jnp.where(y > 0.0, y * lax.rsqrt(y), 0.0)
            a_ref[d, chunk_rows, :] = a
            b_ref[d, chunk_rows, :] = root * (th[:, (2 + d) * w:(3 + d) * w] + 1.0) * xh
        return carry
    lax.fori_loop(0, n_chunks, gates, 0, unroll=GATE_UNROLL)

    def step_rows(d, t):
        pos = t if d == 0 else SCAN_CHUNK - 1 - t
        return pl.ds(pos * pitch, n_chunks)

    def pass1(t, carry):
        hs, ps = carry
        a = [a_ref[d, step_rows(d, t), :] for d in range(2)]
        return (tuple(a[d] * hs[d] + b_ref[d, step_rows(d, t), :] for d in range(2)),
                tuple(a[d] * ps[d] for d in range(2)))

    zero = jnp.zeros((n_chunks, w), F32)
    he, pe = lax.fori_loop(0, SCAN_CHUNK, pass1, ((zero, zero), (zero + 1.0, zero + 1.0)),
                           unroll=SCAN_UNROLL)

    for d in range(2):
        order = range(n_chunks) if d == 0 else range(n_chunks - 1, -1, -1)
        carry = jnp.zeros((1, w), F32)
        for j in order:
            cr_ref[d, j:j + 1, :] = carry
            carry = he[d][j:j + 1, :] + pe[d][j:j + 1, :] * carry

    def pass2(t, hs):
        nh = []
        for d in range(2):
            h = a_ref[d, step_rows(d, t), :] * hs[d] + b_ref[d, step_rows(d, t), :]
            h_ref[d, step_rows(d, t), :] = h
            nh.append(h)
        return tuple(nh)

    lax.fori_loop(0, SCAN_CHUNK, pass2, (cr_ref[0], cr_ref[1]), unroll=SCAN_UNROLL)

    def combine(c, carry):
        r = pl.multiple_of(c * SCAN_CHUNK, SCAN_CHUNK)
        chunk_rows = pl.ds(c, SCAN_CHUNK, stride=pitch)
        h = h_ref[0, chunk_rows, :] + h_ref[1, chunk_rows, :]
        y = yg_ref[pl.ds(r, SCAN_CHUNK), :].astype(F32)
        o_ref[pl.ds(r, SCAN_CHUNK), :] = (h * jax.nn.gelu(y)).astype(o_ref.dtype)
        return carry
    lax.fori_loop(0, n_chunks, combine, 0)


def _rglru(slabs, conv_w, conv_b, wg, bg, lam, *, d_rec):
    _, b, s, w = slabs.shape
    nb = d_rec // w
    assert s % (SCAN_CHUNK * V7X_SUBLANES) == 0
    n_chunks = s // SCAN_CHUNK
    scan_buf = pltpu.VMEM((2, SCAN_CHUNK * _scan_pitch(n_chunks), w), F32)
    est = (3 * 2 * s * w * 2 + (s + 2 * V7X_BF16_ROWS) * w * 4
           + 3 * 2 * SCAN_CHUNK * _scan_pitch(n_chunks) * w * 4)
    return pl.pallas_call(
        functools.partial(_rglru_kernel, seq=s),
        out_shape=jax.ShapeDtypeStruct((b, s, d_rec), BF16),
        grid=(b, nb),
        in_specs=[pl.BlockSpec((None, None, s, w), lambda i, n: (n, i, 0, 0)),
                  pl.BlockSpec((None, None, s, w), lambda i, n: (nb + n, i, 0, 0)),
                  pl.BlockSpec((CONV_W, w), lambda i, n: (0, n)),
                  pl.BlockSpec((1, w), lambda i, n: (0, n)),
                  pl.BlockSpec((None, w, 4 * w), lambda i, n: (n, 0, 0)),
                  pl.BlockSpec((None, 1, 4 * w), lambda i, n: (n, 0, 0)),
                  pl.BlockSpec((2, w), lambda i, n: (0, n))],
        out_specs=pl.BlockSpec((None, s, w), lambda i, n: (i, 0, n)),
        scratch_shapes=[pltpu.VMEM((s + 2 * V7X_BF16_ROWS, w), F32),
                        scan_buf, scan_buf, scan_buf, pltpu.VMEM((2, n_chunks, w), F32)],
        compiler_params=_params(("parallel", "arbitrary"), est),
        name="rglru",
    )(slabs, slabs, conv_w, conv_b, wg, bg, lam)


def _outproj_kernel(a_ref, r_ref, x_ref, ga_ref, gr_ref, w_ref, gp_ref, o_ref, cat_ref):
    da = a_ref.shape[1]
    cat_ref[:, :da] = _rms(a_ref[...].astype(F32), ga_ref[...]).astype(BF16)
    cat_ref[:, da:] = _rms(r_ref[...].astype(F32), gr_ref[...]).astype(BF16)
    mixed = jnp.dot(cat_ref[...], w_ref[...], preferred_element_type=F32)
    o_ref[...] = x_ref[...] + _rms(mixed, gp_ref[...])


def _out_proj(attn, rec, x2, g_attn, g_rec, w, g_post, *, tm):
    t, d = x2.shape
    da, dr = attn.shape[1], rec.shape[1]
    est = 2 * tm * (da + dr) * 2 + 4 * tm * d * 4 + (da + dr) * d * 2 + tm * (da + dr) * 2
    row = lambda i: (i, 0)
    fixed = lambda i: (0, 0)
    return pl.pallas_call(
        _outproj_kernel,
        out_shape=jax.ShapeDtypeStruct((t, d), F32),
        grid=(t // tm,),
        in_specs=[pl.BlockSpec((tm, da), row), pl.BlockSpec((tm, dr), row), pl.BlockSpec((tm, d), row),
                  pl.BlockSpec((1, da), fixed), pl.BlockSpec((1, dr), fixed),
                  pl.BlockSpec((da + dr, d), fixed, pipeline_mode=pl.Buffered(1)),
                  pl.BlockSpec((1, d), fixed)],
        out_specs=pl.BlockSpec((tm, d), row),
        scratch_shapes=[pltpu.VMEM((tm, da + dr), BF16)],
        compiler_params=_params(("parallel",), est),
        name="out_proj",
    )(attn, rec, x2, g_attn, g_rec, w, g_post)


def _ffn_kernel(h_ref, gpre_ref, wg_ref, wu_ref, wd_ref, gpost_ref, o_ref, fn_ref, acc_ref, *, row_chunk):
    f = pl.program_id(1)

    @pl.when(f == 0)
    def _():
        def body(c, carry):
            r = pl.multiple_of(c * row_chunk, row_chunk)
            fn_ref[pl.ds(r, row_chunk), :] = _rms(h_ref[pl.ds(r, row_chunk), :], gpre_ref[...]).astype(BF16)
            acc_ref[pl.ds(r, row_chunk), :] = jnp.zeros((row_chunk, acc_ref.shape[1]), acc_ref.dtype)
            return carry
        lax.fori_loop(0, h_ref.shape[0] // row_chunk, body, 0, unroll=2)

    fn = fn_ref[...]
    acts = []
    for c in range(0, wg_ref.shape[1], V7X_MXU_COLS):
        gate = jnp.dot(fn, wg_ref[:, c:c + V7X_MXU_COLS], preferred_element_type=F32)
        up = jnp.dot(fn, wu_ref[:, c:c + V7X_MXU_COLS], preferred_element_type=F32)
        acts.append((jax.nn.silu(gate) * up).astype(BF16))
    for k in range(len(acts)):
        acc_ref[...] += jnp.dot(acts[k], wd_ref[k * V7X_MXU_COLS:(k + 1) * V7X_MXU_COLS, :],
                                preferred_element_type=F32)

    @pl.when(f == pl.num_programs(1) - 1)
    def _():
        def body(c, carry):
            r = pl.multiple_of(c * row_chunk, row_chunk)
            o_ref[pl.ds(r, row_chunk), :] = _rms(acc_ref[pl.ds(r, row_chunk), :], gpost_ref[...]).astype(o_ref.dtype)
            return carry
        lax.fori_loop(0, h_ref.shape[0] // row_chunk, body, 0, unroll=2)


def _ffn(h1, g_pre, w_gate, w_up, w_down, g_post, *, tm, tf):
    t, d = h1.shape
    dff = w_gate.shape[1]
    est = 2 * tm * d * 4 + 3 * 2 * d * tf * 2 + 2 * tm * d * 2 + tm * d * 2 + tm * d * 4
    return pl.pallas_call(
        functools.partial(_ffn_kernel, row_chunk=min(128, tm)),
        out_shape=jax.ShapeDtypeStruct((t, d), BF16),
        grid=(t // tm, dff // tf),
        in_specs=[pl.BlockSpec((tm, d), lambda i, f: (i, 0)),
                  pl.BlockSpec((1, d), lambda i, f: (0, 0)),
                  pl.BlockSpec((d, tf), lambda i, f: (0, f)),
                  pl.BlockSpec((d, tf), lambda i, f: (0, f)),
                  pl.BlockSpec((tf, d), lambda i, f: (f, 0)),
                  pl.BlockSpec((1, d), lambda i, f: (0, 0))],
        out_specs=pl.BlockSpec((tm, d), lambda i, f: (i, 0)),
        scratch_shapes=[pltpu.VMEM((tm, d), BF16), pltpu.VMEM((tm, d), F32)],
        compiler_params=_params(("parallel", "arbitrary"), est),
        name="ffn",
    )(h1, g_pre, w_gate, w_up, w_down, g_post)


def _ple_kernel(h_ref, d_ref, p_ref, gpre_ref, wg_ref, wp_ref, gpost_ref, o_ref):
    h2 = h_ref[...] + d_ref[...].astype(F32)
    gate = _sigmoid(jnp.dot(_rms(h2, gpre_ref[...]).astype(BF16), wg_ref[...],
                            preferred_element_type=F32))
    ple = jnp.dot(p_ref[...].astype(BF16), wp_ref[...], preferred_element_type=F32)
    o_ref[...] = h2 + _rms(gate * ple, gpost_ref[...])


def _ple(h1, dff, p2, g_pre, w_gate, w_proj, g_post, *, tm):
    t, d = h1.shape
    dp = p2.shape[1]
    est = 4 * tm * d * 4 + 2 * tm * d * 2 + 2 * tm * dp * 4 + d * d * 2 + 2 * dp * d * 2 + 3 * tm * d * 4
    row = lambda i: (i, 0)
    fixed = lambda i: (0, 0)
    return pl.pallas_call(
        _ple_kernel,
        out_shape=jax.ShapeDtypeStruct((t, d), F32),
        grid=(t // tm,),
        in_specs=[pl.BlockSpec((tm, d), row), pl.BlockSpec((tm, d), row), pl.BlockSpec((tm, dp), row),
                  pl.BlockSpec((1, d), fixed),
                  pl.BlockSpec((d, d), fixed, pipeline_mode=pl.Buffered(1)),
                  pl.BlockSpec((dp, d), fixed),
                  pl.BlockSpec((1, d), fixed)],
        out_specs=pl.BlockSpec((tm, d), row),
        compiler_params=_params(("parallel",), est),
        name="ple",
    )(h1, dff, p2, g_pre, w_gate, w_proj, g_post)


def _layer(h, p_i, g_mix_pre, w_in, rpb, conv_w, conv_b, w_rg_a, b_rg_a, w_rg_i, b_rg_i, lam,
           g_attn_out, g_rec_out, w_out, g_mix_post, g_ffn_pre, w_ffn_gate, w_ffn_up, w_ffn_down,
           g_ffn_post, g_ple_pre, w_ple_gate, w_ple_proj, g_ple_post):
    batch, seq, d = h.shape
    t = batch * seq
    d_attn = g_attn_out.shape[0]
    d_rec = g_rec_out.shape[0]
    nb = d_rec // REC_BLOCK_W
    row = lambda v: v.reshape(1, -1).astype(F32)

    x2 = h.reshape(t, d)
    u, u_rec, (w_out_b, w_gate_b, w_up_b, w_down_b, w_pgate_b) = _in_proj(
        x2, row(g_mix_pre), w_in.astype(BF16), [w_out, w_ffn_gate, w_ffn_up, w_ffn_down, w_ple_gate],
        n_main=3 * d_attn, tm=min(512, t), tn=1024)

    attn = _attention(u, _attn_bias(rpb, seq // GRID_W), batch=batch, seq=seq, d_attn=d_attn)

    wg = (0.5 * jnp.concatenate([w_rg_a[0], w_rg_a[1], w_rg_i[0], w_rg_i[1]], axis=-1)).astype(BF16)
    bg = jnp.concatenate([b_rg_a.reshape(2, nb, 1, REC_BLOCK_W)[0], b_rg_a.reshape(2, nb, 1, REC_BLOCK_W)[1],
                          b_rg_i.reshape(2, nb, 1, REC_BLOCK_W)[0], b_rg_i.reshape(2, nb, 1, REC_BLOCK_W)[1]],
                         axis=-1).astype(F32) * 0.5
    rec = _rglru(u_rec.reshape(-1, batch, seq, REC_BLOCK_W), conv_w.astype(F32), row(conv_b), wg, bg,
                 lam.astype(F32), d_rec=d_rec)

    h1 = _out_proj(attn, rec.reshape(t, d_rec), x2, row(g_attn_out), row(g_rec_out),
                   w_out_b, row(g_mix_post), tm=min(512, t))

    dff = _ffn(h1, row(g_ffn_pre), w_gate_b, w_up_b,
               w_down_b, row(g_ffn_post), tm=min(1024, t), tf=512)

    out = _ple(h1, dff, p_i.reshape(t, -1), row(g_ple_pre), w_pgate_b,
               w_ple_proj.astype(BF16), row(g_ple_post), tm=min(512, t))
    return out.reshape(batch, seq, d)


def kernel(x, p, g_mix_pre, w_in, rpb, conv_w, conv_b, w_rg_a, b_rg_a, w_rg_i, b_rg_i, lam, g_attn_out, g_rec_out, w_out, g_mix_post, g_ffn_pre, w_ffn_gate, w_ffn_up, w_ffn_down, g_ffn_post, g_ple_pre, w_ple_gate, w_ple_proj, g_ple_post):
    h = x
    for i in range(p.shape[0]):
        h = _layer(h, p[i], g_mix_pre[i], w_in[i], rpb[i], conv_w[i], conv_b[i], w_rg_a[i], b_rg_a[i],
                   w_rg_i[i], b_rg_i[i], lam[i], g_attn_out[i], g_rec_out[i], w_out[i], g_mix_post[i],
                   g_ffn_pre[i], w_ffn_gate[i], w_ffn_up[i], w_ffn_down[i], g_ffn_post[i], g_ple_pre[i],
                   w_ple_gate[i], w_ple_proj[i], g_ple_post[i])
    return h
```

```python
import functools

import numpy as np
import jax
import jax.numpy as jnp
from jax import lax
from jax.experimental import pallas as pl
from jax.experimental.pallas import tpu as pltpu

F32 = jnp.float32
BF16 = jnp.bfloat16

EPS = 1e-6
NEG_INF = -1e9
LOG2_E = 1.4426950408889634
GRID_W = 64
HEAD_DIM = 128
WIN_R = 8
WIN_C = 16
CONV_W = 4
C_RG = 8.0
REC_BLOCK_W = 128

V7X_SUBLANES = 8
V7X_MXU_COLS = 256
V7X_BF16_ROWS = 16
V7X_VMEM_BYTES = 64 * 1024 * 1024
V7X_VMEM_CAP = V7X_VMEM_BYTES - 8 * 1024 * 1024

Q_ROWS = 4
Q_TOK = Q_ROWS * GRID_W
K_BLOCKS = 3

SCAN_CHUNK = 256
GATE_UNROLL = 8
SCAN_UNROLL = 8


def _vmem_limit(estimate_bytes):
    return int(min(estimate_bytes * 5 // 4 + (4 << 20), V7X_VMEM_CAP))


def _params(semantics, estimate_bytes):
    return pltpu.CompilerParams(dimension_semantics=semantics,
                                vmem_limit_bytes=_vmem_limit(estimate_bytes))


def _rms(x, g):
    ms = jnp.mean(x * x, axis=-1, keepdims=True)
    return x * lax.rsqrt(ms + EPS) * g


def _inproj_kernel(*refs, tn, n_riders):
    x_ref, g_ref, w_ref = refs[:3]
    rider_src = refs[3:3 + n_riders]
    o_ref, slab_ref = refs[3 + n_riders:5 + n_riders]
    rider_dst = refs[5 + n_riders:5 + 2 * n_riders]
    hn_ref = refs[5 + 2 * n_riders]
    for src, dst in zip(rider_src, rider_dst):
        dst[...] = src[...].astype(dst.dtype)
    hn_ref[...] = _rms(x_ref[...], g_ref[...]).astype(BF16)
    n_main = o_ref.shape[1]
    w = REC_BLOCK_W
    for c in range(0, w_ref.shape[1], tn):
        res = jnp.dot(hn_ref[...], w_ref[:, c:c + tn], preferred_element_type=F32).astype(o_ref.dtype)
        if c < n_main:
            o_ref[:, c:c + tn] = res
        else:
            for k in range(tn // w):
                slab_ref[(c - n_main) // w + k] = res[:, k * w:(k + 1) * w]


def _rider_block(rows, steps):
    share = 1
    while (rows * share) % (steps * V7X_BF16_ROWS) != 0:
        share *= 2
    return rows * share // steps, share


def _in_proj(x2, g, w, riders, *, n_main, tm, tn):
    t, d = x2.shape
    n = w.shape[1]
    assert n_main % tn == 0 and (n - n_main) % tn == 0 and tn % REC_BLOCK_W == 0
    n_slabs = (n - n_main) // REC_BLOCK_W
    steps = t // tm
    est = 2 * tm * d * 4 + d * n * 2 + 2 * tm * n * 2 + tm * d * 2 + tm * d * 4
    rider_specs = []
    for r in riders:
        rows, share = _rider_block(r.shape[0], steps)
        rider_specs.append(pl.BlockSpec((rows, r.shape[1]), lambda i, share=share: (i // share, 0)))
        est += 2 * rows * r.shape[1] * (4 + 2)
    out = pl.pallas_call(
        functools.partial(_inproj_kernel, tn=tn, n_riders=len(riders)),
        out_shape=[jax.ShapeDtypeStruct((t, n_main), BF16),
                   jax.ShapeDtypeStruct((n_slabs, t, REC_BLOCK_W), BF16)]
                  + [jax.ShapeDtypeStruct(r.shape, BF16) for r in riders],
        grid=(steps,),
        in_specs=[pl.BlockSpec((tm, d), lambda i: (i, 0)),
                  pl.BlockSpec((1, d), lambda i: (0, 0)),
                  pl.BlockSpec((d, n), lambda i: (0, 0), pipeline_mode=pl.Buffered(1))] + rider_specs,
        out_specs=[pl.BlockSpec((tm, n_main), lambda i: (i, 0)),
                   pl.BlockSpec((n_slabs, tm, REC_BLOCK_W), lambda i: (0, i, 0))] + rider_specs,
        scratch_shapes=[pltpu.VMEM((tm, d), BF16)],
        compiler_params=_params(("arbitrary",), est),
        name="in_proj",
    )(x2, g, w, *riders)
    return out[0], out[1], out[2:]


def _attn_window(rows):
    nblk = rows // Q_ROWS
    kr = min(WIN_R, rows)
    n_dr = 2 * WIN_R - 1
    q_row0 = np.array([0, Q_ROWS, rows - Q_ROWS])
    k_row0 = np.array([0, 0, (nblk - K_BLOCKS) * Q_ROWS])
    qr = q_row0[:, None] + np.arange(Q_ROWS)[None, :]
    kro = k_row0[:, None] + np.arange(K_BLOCKS * Q_ROWS)[None, :]
    rstart = np.clip(qr - kr // 2, 0, rows - kr)
    row_ok = (kro[:, None, :] >= rstart[:, :, None]) & (kro[:, None, :] < rstart[:, :, None] + kr)
    dr = np.clip(kro[:, None, :] - qr[:, :, None] + WIN_R - 1, 0, n_dr - 1)
    return row_ok, dr


def _bias_kernel(rpb_ref, o_ref, *, row_ok, dr):
    lanes = 2 * GRID_W
    qc = lax.broadcasted_iota(jnp.int32, (GRID_W, lanes), 0)
    lane = lax.broadcasted_iota(jnp.int32, (GRID_W, lanes), 1)
    kc = lane & (GRID_W - 1)
    cstart = jnp.clip(qc - WIN_C // 2, 0, GRID_W - WIN_C)
    col_ok = (kc >= cstart) & (kc < cstart + WIN_C)
    first = lane < GRID_W
    neg = jnp.full((GRID_W, lanes), NEG_INF, F32)

    toeplitz = {}

    def block(r, second):
        if (r, second) not in toeplitz:
            row = jnp.broadcast_to(rpb_ref[r:r + 1, :], (GRID_W, lanes))
            t = pltpu.roll(row, lanes - (WIN_C - 1), 1, stride=1, stride_axis=0)
            toeplitz[(r, False)] = t
            toeplitz[(r, True)] = pltpu.roll(t, GRID_W, 1)
        return toeplitz[(r, second)]

    for t in range(3):
        for a in range(Q_ROWS):
            for pair in range(K_BLOCKS * Q_ROWS // 2):
                l0, l1 = 2 * pair, 2 * pair + 1
                left = block(int(dr[t, a, l0]), False) if row_ok[t, a, l0] else neg
                right = block(int(dr[t, a, l1]), True) if row_ok[t, a, l1] else neg
                tile = jnp.where(col_ok, jnp.where(first, left, right), neg)
                o_ref[t, a * GRID_W:(a + 1) * GRID_W, pair * lanes:(pair + 1) * lanes] = tile


def _attn_bias(rpb, rows):
    n_heads, n_dr, n_dc = rpb.shape
    row_ok, dr = _attn_window(rows)
    padded = jnp.zeros((n_heads, -(-n_dr // V7X_SUBLANES) * V7X_SUBLANES, 2 * GRID_W), F32)
    padded = padded.at[:, :n_dr, :n_dc].set(rpb.astype(F32))
    shape = (3, n_heads, Q_TOK, K_BLOCKS * Q_TOK)
    return pl.pallas_call(
        functools.partial(_bias_kernel, row_ok=row_ok, dr=dr),
        out_shape=jax.ShapeDtypeStruct(shape, F32),
        grid=(n_heads,),
        in_specs=[pl.BlockSpec((None,) + padded.shape[1:], lambda h: (h, 0, 0))],
        out_specs=pl.BlockSpec((3, None) + shape[2:], lambda h: (0, h, 0, 0)),
        compiler_params=_params(("parallel",), 2 * 3 * shape[2] * shape[3] * 4),
        name="attn_bias",
    )(padded)


def _attn_heads(q_ref, k_refs, v_refs, bias_ref, o_ref, *, n_heads, need):
    scale = HEAD_DIM ** -0.5
    n_tiles = len(need[0])
    per_blk = n_tiles // K_BLOCKS
    zeros = jnp.zeros((GRID_W, HEAD_DIM), BF16)
    for h in range(n_heads):
        hs = slice(h * HEAD_DIM, (h + 1) * HEAD_DIM)
        q = (q_ref[:, hs].astype(F32) * scale).astype(BF16)
        s = [lax.dot_general(q, k_refs[i][:, hs], (((1,), (1,)), ((), ())),
                             preferred_element_type=F32) for i in range(K_BLOCKS)]
        p_rows, l_rows = [], []
        for a in range(Q_ROWS):
            rs = slice(a * GRID_W, (a + 1) * GRID_W)
            tiles = {t: s[t // per_blk][rs, (t % per_blk) * HEAD_DIM:(t % per_blk + 1) * HEAD_DIM]
                        + bias_ref[h, rs, t * HEAD_DIM:(t + 1) * HEAD_DIM]
                     for t in range(n_tiles) if need[a][t]}
            m = functools.reduce(jnp.maximum, tiles.values()).max(axis=-1, keepdims=True)
            p = {t: jnp.exp(v - m) for t, v in tiles.items()}
            l_rows.append(functools.reduce(jnp.add, p.values()).sum(axis=-1, keepdims=True))
            p_rows.append([p[t].astype(BF16) if t in p else zeros for t in range(n_tiles)])
        acc = None
        for i in range(K_BLOCKS):
            p_blk = jnp.concatenate(
                [jnp.concatenate(row[i * per_blk:(i + 1) * per_blk], axis=1) for row in p_rows], axis=0)
            part = jnp.dot(p_blk, v_refs[i][:, hs], preferred_element_type=F32)
            acc = part if acc is None else acc + part
        o_ref[:, hs] = (acc * (1.0 / jnp.concatenate(l_rows, axis=0))).astype(o_ref.dtype)


def _attn_kernel(q_ref, k0_ref, k1_ref, k2_ref, v0_ref, v1_ref, v2_ref, bias_ref, o_ref, *,
                 n_heads, need):
    j = pl.program_id(0)
    last = pl.num_programs(0) - 1
    block_type = jnp.where(j == 0, 0, jnp.where(j == last, 2, 1))
    for t in range(3):
        @pl.when(block_type == t)
        def _():
            _attn_heads(q_ref, (k0_ref, k1_ref, k2_ref), (v0_ref, v1_ref, v2_ref), bias_ref, o_ref,
                        n_heads=n_heads, need=need[t])


def _attention(u, bias, *, batch, seq, d_attn):
    t = u.shape[0]
    row_ok, _ = _attn_window(seq // GRID_W)
    pairs = row_ok.reshape(3, Q_ROWS, -1, HEAD_DIM // GRID_W)
    need = tuple(tuple(tuple(bool(x) for x in row) for row in typ) for typ in pairs.any(axis=-1))
    n_heads = d_attn // HEAD_DIM
    nblk = seq // Q_TOK
    assert nblk >= K_BLOCKS

    def q_map(j, b):
        return (b * nblk + j, 0)

    def kv_map(i, col):
        def f(j, b):
            return (b * nblk + jnp.clip(j - 1, 0, nblk - K_BLOCKS) + i, col)
        return f

    def bias_map(j, b):
        return (jnp.where(j == 0, 0, jnp.where(j == nblk - 1, 2, 1)), 0, 0, 0)

    blk = pl.BlockSpec((Q_TOK, d_attn), q_map)
    in_specs = [blk]
    in_specs += [pl.BlockSpec((Q_TOK, d_attn), kv_map(i, 1)) for i in range(K_BLOCKS)]
    in_specs += [pl.BlockSpec((Q_TOK, d_attn), kv_map(i, 2)) for i in range(K_BLOCKS)]
    in_specs += [pl.BlockSpec((None, n_heads, Q_TOK, K_BLOCKS * Q_TOK), bias_map)]
    est = 2 * 8 * Q_TOK * d_attn * 2 + 2 * n_heads * Q_TOK * K_BLOCKS * Q_TOK * 4
    return pl.pallas_call(
        functools.partial(_attn_kernel, n_heads=n_heads, need=need),
        out_shape=jax.ShapeDtypeStruct((t, d_attn), BF16),
        grid=(nblk, batch),
        in_specs=in_specs,
        out_specs=blk,
        compiler_params=_params(("arbitrary", "arbitrary"), est),
        name="nattn",
    )(u, u, u, u, u, u, u, bias)


def _scan_pitch(n_chunks):
    return n_chunks + 4 if (n_chunks // 4) % 2 == 0 else n_chunks + 8


def _sigmoid(x):
    return 0.5 * jnp.tanh(0.5 * x) + 0.5


def _rglru_kernel(xr_ref, yg_ref, cw_ref, cb_ref, wg_ref, bg_ref, lam_ref, o_ref,
                  xp_ref, a_ref, b_ref, h_ref, cr_ref, *, seq):
    n_chunks = seq // SCAN_CHUNK
    pitch = _scan_pitch(n_chunks)
    w = REC_BLOCK_W
    halo = V7X_BF16_ROWS

    zeros = jnp.zeros((halo, w), xp_ref.dtype)
    xp_ref[0:halo, :] = zeros
    xp_ref[seq + halo:seq + 2 * halo, :] = zeros

    def stage(c, carry):
        r = pl.multiple_of(c * SCAN_CHUNK, SCAN_CHUNK)
        xp_ref[pl.ds(r + halo, SCAN_CHUNK), :] = xr_ref[pl.ds(r, SCAN_CHUNK), :].astype(F32)
        return carry
    lax.fori_loop(0, n_chunks, stage, 0)

    z = -lam_ref[...]
    half_coef = (-0.5 * C_RG * LOG2_E) * (jnp.maximum(z, 0.0) + jnp.log1p(jnp.exp(-jnp.abs(z))))

    left = CONV_W // 2

    def gates(c, carry):
        r = pl.multiple_of(c * SCAN_CHUNK, SCAN_CHUNK)
        xc = cb_ref[...]
        for j in range(CONV_W):
            xc = xc + xp_ref[pl.ds(r + (halo + j - left), SCAN_CHUNK), :] * cw_ref[j:j + 1, :]
        th = jnp.tanh(jnp.dot(xc.astype(BF16), wg_ref[...], preferred_element_type=F32) + bg_ref[...])
        xh = 0.5 * xc
        chunk_rows = pl.ds(c, SCAN_CHUNK, stride=pitch)
        for d in range(2):
            a = jnp.exp2(th[:, d * w:(d + 1) * w] * half_coef[d:d + 1, :] + half_coef[d:d + 1, :])
            y = 1.0 - a * a
            root = jnp.where(y > 0.0, y * lax.rsqrt(y), 0.0)
            a_ref[d, chunk_rows, :] = a
            b_ref[d, chunk_rows, :] = root * (th[:, (2 + d) * w:(3 + d) * w] + 1.0) * xh
        return carry
    lax.fori_loop(0, n_chunks, gates, 0, unroll=GATE_UNROLL)

    def step_rows(d, t):
        pos = t if d == 0 else SCAN_CHUNK - 1 - t
        return pl.ds(pos * pitch, n_chunks)

    def pass1(t, carry):
        hs, ps = carry
        a = [a_ref[d, step_rows(d, t), :] for d in range(2)]
        return (tuple(a[d] * hs[d] + b_ref[d, step_rows(d, t), :] for d in range(2)),
                tuple(a[d] * ps[d] for d in range(2)))

    zero = jnp.zeros((n_chunks, w), F32)
    he, pe = lax.fori_loop(0, SCAN_CHUNK, pass1, ((zero, zero), (zero + 1.0, zero + 1.0)),
                           unroll=SCAN_UNROLL)

    for d in range(2):
        order = range(n_chunks) if d == 0 else range(n_chunks - 1, -1, -1)
        carry = jnp.zeros((1, w), F32)
        for j in order:
            cr_ref[d, j:j + 1, :] = carry
            carry = he[d][j:j + 1, :] + pe[d][j:j + 1, :] * carry

    def pass2(t, hs):
        nh = []
        for d in range(2):
            h = a_ref[d, step_rows(d, t), :] * hs[d] + b_ref[d, step_rows(d, t), :]
            h_ref[d, step_rows(d, t), :] = h
            nh.append(h)
        return tuple(nh)

    lax.fori_loop(0, SCAN_CHUNK, pass2, (cr_ref[0], cr_ref[1]), unroll=SCAN_UNROLL)

    def combine(c, carry):
        r = pl.multiple_of(c * SCAN_CHUNK, SCAN_CHUNK)
        chunk_rows = pl.ds(c, SCAN_CHUNK, stride=pitch)
        h = h_ref[0, chunk_rows, :] + h_ref[1, chunk_rows, :]
        y = yg_ref[pl.ds(r, SCAN_CHUNK), :].astype(F32)
        o_ref[pl.ds(r, SCAN_CHUNK), :] = (h * jax.nn.gelu(y)).astype(o_ref.dtype)
        return carry
    lax.fori_loop(0, n_chunks, combine, 0)


def _rglru(slabs, conv_w, conv_b, wg, bg, lam, *, d_rec):
    _, b, s, w = slabs.shape
    nb = d_rec // w
    assert s % (SCAN_CHUNK * V7X_SUBLANES) == 0
    n_chunks = s // SCAN_CHUNK
    scan_buf = pltpu.VMEM((2, SCAN_CHUNK * _scan_pitch(n_chunks), w), F32)
    est = (3 * 2 * s * w * 2 + (s + 2 * V7X_BF16_ROWS) * w * 4
           + 3 * 2 * SCAN_CHUNK * _scan_pitch(n_chunks) * w * 4)
    return pl.pallas_call(
        functools.partial(_rglru_kernel, seq=s),
        out_shape=jax.ShapeDtypeStruct((b, s, d_rec), BF16),
        grid=(b, nb),
        in_specs=[pl.BlockSpec((None, None, s, w), lambda i, n: (n, i, 0, 0)),
                  pl.BlockSpec((None, None, s, w), lambda i, n: (nb + n, i, 0, 0)),
                  pl.BlockSpec((CONV_W, w), lambda i, n: (0, n)),
                  pl.BlockSpec((1, w), lambda i, n: (0, n)),
                  pl.BlockSpec((None, w, 4 * w), lambda i, n: (n, 0, 0)),
                  pl.BlockSpec((None, 1, 4 * w), lambda i, n: (n, 0, 0)),
                  pl.BlockSpec((2, w), lambda i, n: (0, n))],
        out_specs=pl.BlockSpec((None, s, w), lambda i, n: (i, 0, n)),
        scratch_shapes=[pltpu.VMEM((s + 2 * V7X_BF16_ROWS, w), F32),
                        scan_buf, scan_buf, scan_buf, pltpu.VMEM((2, n_chunks, w), F32)],
        compiler_params=_params(("parallel", "arbitrary"), est),
        name="rglru",
    )(slabs, slabs, conv_w, conv_b, wg, bg, lam)


def _outproj_kernel(a_ref, r_ref, x_ref, ga_ref, gr_ref, w_ref, gp_ref, o_ref, cat_ref):
    da = a_ref.shape[1]
    cat_ref[:, :da] = _rms(a_ref[...].astype(F32), ga_ref[...]).astype(BF16)
    cat_ref[:, da:] = _rms(r_ref[...].astype(F32), gr_ref[...]).astype(BF16)
    mixed = jnp.dot(cat_ref[...], w_ref[...], preferred_element_type=F32)
    o_ref[...] = x_ref[...] + _rms(mixed, gp_ref[...])


def _out_proj(attn, rec, x2, g_attn, g_rec, w, g_post, *, tm):
    t, d = x2.shape
    da, dr = attn.shape[1], rec.shape[1]
    est = 2 * tm * (da + dr) * 2 + 4 * tm * d * 4 + (da + dr) * d * 2 + tm * (da + dr) * 2
    row = lambda i: (i, 0)
    fixed = lambda i: (0, 0)
    return pl.pallas_call(
        _outproj_kernel,
        out_shape=jax.ShapeDtypeStruct((t, d), F32),
        grid=(t // tm,),
        in_specs=[pl.BlockSpec((tm, da), row), pl.BlockSpec((tm, dr), row), pl.BlockSpec((tm, d), row),
                  pl.BlockSpec((1, da), fixed), pl.BlockSpec((1, dr), fixed),
                  pl.BlockSpec((da + dr, d), fixed, pipeline_mode=pl.Buffered(1)),
                  pl.BlockSpec((1, d), fixed)],
        out_specs=pl.BlockSpec((tm, d), row),
        scratch_shapes=[pltpu.VMEM((tm, da + dr), BF16)],
        compiler_params=_params(("parallel",), est),
        name="out_proj",
    )(attn, rec, x2, g_attn, g_rec, w, g_post)


def _ffn_kernel(h_ref, gpre_ref, wg_ref, wu_ref, wd_ref, gpost_ref, o_ref, fn_ref, acc_ref, *, row_chunk):
    f = pl.program_id(1)

    @pl.when((f == 0) & (pl.program_id(0) == 0))
    def _():
        acc_ref[...] = jnp.zeros(acc_ref.shape, acc_ref.dtype)

    @pl.when(f == 0)
    def _():
        def body(c, carry):
            r = pl.multiple_of(c * row_chunk, row_chunk)
            fn_ref[pl.ds(r, row_chunk), :] = _rms(h_ref[pl.ds(r, row_chunk), :], gpre_ref[...]).astype(BF16)
            return carry
        lax.fori_loop(0, h_ref.shape[0] // row_chunk, body, 0, unroll=2)

    fn = fn_ref[...]
    acts = []
    for c in range(0, wg_ref.shape[1], V7X_MXU_COLS):
        gate = jnp.dot(fn, wg_ref[:, c:c + V7X_MXU_COLS], preferred_element_type=F32)
        up = jnp.dot(fn, wu_ref[:, c:c + V7X_MXU_COLS], preferred_element_type=F32)
        acts.append((jax.nn.silu(gate) * up).astype(BF16))
    for k in range(len(acts)):
        part = jnp.dot(acts[k], wd_ref[k * V7X_MXU_COLS:(k + 1) * V7X_MXU_COLS, :], preferred_element_type=F32)
        prev = acc_ref[...]
        if k == 0:
            prev = jnp.where(f > 0, prev, 0.0)
        acc_ref[...] = prev + part

    @pl.when(f == pl.num_programs(1) - 1)
    def _():
        def body(c, carry):
            r = pl.multiple_of(c * row_chunk, row_chunk)
            o_ref[pl.ds(r, row_chunk), :] = _rms(acc_ref[pl.ds(r, row_chunk), :], gpost_ref[...]).astype(o_ref.dtype)
            return carry
        lax.fori_loop(0, h_ref.shape[0] // row_chunk, body, 0, unroll=2)


def _ffn(h1, g_pre, w_gate, w_up, w_down, g_post, *, tm, tf):
    t, d = h1.shape
    dff = w_gate.shape[1]
    est = 2 * tm * d * 4 + 3 * 2 * d * tf * 2 + 2 * tm * d * 2 + tm * d * 2 + tm * d * 4
    return pl.pallas_call(
        functools.partial(_ffn_kernel, row_chunk=min(128, tm)),
        out_shape=jax.ShapeDtypeStruct((t, d), BF16),
        grid=(t // tm, dff // tf),
        in_specs=[pl.BlockSpec((tm, d), lambda i, f: (i, 0)),
                  pl.BlockSpec((1, d), lambda i, f: (0, 0)),
                  pl.BlockSpec((d, tf), lambda i, f: (0, f)),
                  pl.BlockSpec((d, tf), lambda i, f: (0, f)),
                  pl.BlockSpec((tf, d), lambda i, f: (f, 0)),
                  pl.BlockSpec((1, d), lambda i, f: (0, 0))],
        out_specs=pl.BlockSpec((tm, d), lambda i, f: (i, 0)),
        scratch_shapes=[pltpu.VMEM((tm, d), BF16), pltpu.VMEM((tm, d), F32)],
        compiler_params=_params(("parallel", "arbitrary"), est),
        name="ffn",
    )(h1, g_pre, w_gate, w_up, w_down, g_post)


def _ple_kernel(h_ref, d_ref, p_ref, gpre_ref, wg_ref, wp_ref, gpost_ref, o_ref):
    h2 = h_ref[...] + d_ref[...].astype(F32)
    gate = _sigmoid(jnp.dot(_rms(h2, gpre_ref[...]).astype(BF16), wg_ref[...],
                            preferred_element_type=F32))
    ple = jnp.dot(p_ref[...].astype(BF16), wp_ref[...], preferred_element_type=F32)
    o_ref[...] = h2 + _rms(gate * ple, gpost_ref[...])


def _ple(h1, dff, p2, g_pre, w_gate, w_proj, g_post, *, tm):
    t, d = h1.shape
    dp = p2.shape[1]
    est = 4 * tm * d * 4 + 2 * tm * d * 2 + 2 * tm * dp * 4 + d * d * 2 + 2 * dp * d * 2 + 3 * tm * d * 4
    row = lambda i: (i, 0)
    fixed = lambda i: (0, 0)
    return pl.pallas_call(
        _ple_kernel,
        out_shape=jax.ShapeDtypeStruct((t, d), F32),
        grid=(t // tm,),
        in_specs=[pl.BlockSpec((tm, d), row), pl.BlockSpec((tm, d), row), pl.BlockSpec((tm, dp), row),
                  pl.BlockSpec((1, d), fixed),
                  pl.BlockSpec((d, d), fixed, pipeline_mode=pl.Buffered(1)),
                  pl.BlockSpec((dp, d), fixed),
                  pl.BlockSpec((1, d), fixed)],
        out_specs=pl.BlockSpec((tm, d), row),
        compiler_params=_params(("parallel",), est),
        name="ple",
    )(h1, dff, p2, g_pre, w_gate, w_proj, g_post)


def _layer(h, p_i, g_mix_pre, w_in, rpb, conv_w, conv_b, w_rg_a, b_rg_a, w_rg_i, b_rg_i, lam,
           g_attn_out, g_rec_out, w_out, g_mix_post, g_ffn_pre, w_ffn_gate, w_ffn_up, w_ffn_down,
           g_ffn_post, g_ple_pre, w_ple_gate, w_ple_proj, g_ple_post):
    batch, seq, d = h.shape
    t = batch * seq
    d_attn = g_attn_out.shape[0]
    d_rec = g_rec_out.shape[0]
    nb = d_rec // REC_BLOCK_W
    row = lambda v: v.reshape(1, -1).astype(F32)

    x2 = h.reshape(t, d)
    u, u_rec, (w_out_b, w_gate_b, w_up_b, w_down_b, w_pgate_b) = _in_proj(
        x2, row(g_mix_pre), w_in.astype(BF16), [w_out, w_ffn_gate, w_ffn_up, w_ffn_down, w_ple_gate],
        n_main=3 * d_attn, tm=min(512, t), tn=1024)

    attn = _attention(u, _attn_bias(rpb, seq // GRID_W), batch=batch, seq=seq, d_attn=d_attn)

    wg = (0.5 * jnp.concatenate([w_rg_a[0], w_rg_a[1], w_rg_i[0], w_rg_i[1]], axis=-1)).astype(BF16)
    bg = jnp.concatenate([b_rg_a.reshape(2, nb, 1, REC_BLOCK_W)[0], b_rg_a.reshape(2, nb, 1, REC_BLOCK_W)[1],
                          b_rg_i.reshape(2, nb, 1, REC_BLOCK_W)[0], b_rg_i.reshape(2, nb, 1, REC_BLOCK_W)[1]],
                         axis=-1).astype(F32) * 0.5
    rec = _rglru(u_rec.reshape(-1, batch, seq, REC_BLOCK_W), conv_w.astype(F32), row(conv_b), wg, bg,
                 lam.astype(F32), d_rec=d_rec)

    h1 = _out_proj(attn, rec.reshape(t, d_rec), x2, row(g_attn_out), row(g_rec_out),
                   w_out_b, row(g_mix_post), tm=min(512, t))

    dff = _ffn(h1, row(g_ffn_pre), w_gate_b, w_up_b,
               w_down_b, row(g_ffn_post), tm=min(1024, t), tf=512)

    out = _ple(h1, dff, p_i.reshape(t, -1), row(g_ple_pre), w_pgate_b,
               w_ple_proj.astype(BF16), row(g_ple_post), tm=min(512, t))
    return out.reshape(batch, seq, d)


def kernel(x, p, g_mix_pre, w_in, rpb, conv_w, conv_b, w_rg_a, b_rg_a, w_rg_i, b_rg_i, lam, g_attn_out, g_rec_out, w_out, g_mix_post, g_ffn_pre, w_ffn_gate, w_ffn_up, w_ffn_down, g_ffn_post, g_ple_pre, w_ple_gate, w_ple_proj, g_ple_post):
    h = x
    for i in range(p.shape[0]):
        h = _layer(h, p[i], g_mix_pre[i], w_in[i], rpb[i], conv_w[i], conv_b[i], w_rg_a[i], b_rg_a[i],
                   w_rg_i[i], b_rg_i[i], lam[i], g_attn_out[i], g_rec_out[i], w_out[i], g_mix_post[i],
                   g_ffn_pre[i], w_ffn_gate[i], w_ffn_up[i], w_ffn_down[i], g_ffn_post[i], g_ple_pre[i],
                   w_ple_gate[i], w_ple_proj[i], g_ple_post[i])
    return h
```

```python
import functools

import numpy as np
import jax
import jax.numpy as jnp
from jax import lax
from jax.experimental import pallas as pl
from jax.experimental.pallas import tpu as pltpu

F32 = jnp.float32
BF16 = jnp.bfloat16

EPS = 1e-6
NEG_INF = -1e9
LOG2_E = 1.4426950408889634
GRID_W = 64
HEAD_DIM = 128
WIN_R = 8
WIN_C = 16
CONV_W = 4
C_RG = 8.0
REC_BLOCK_W = 128

V7X_SUBLANES = 8
V7X_MXU_COLS = 256
V7X_BF16_ROWS = 16
V7X_VMEM_BYTES = 64 * 1024 * 1024
V7X_VMEM_CAP = V7X_VMEM_BYTES - 8 * 1024 * 1024

Q_ROWS = 4
Q_TOK = Q_ROWS * GRID_W
K_BLOCKS = 3

SCAN_CHUNK = 256
GATE_UNROLL = 8
SCAN_UNROLL = 8


def _vmem_limit(estimate_bytes):
    return int(min(estimate_bytes * 5 // 4 + (4 << 20), V7X_VMEM_CAP))


def _params(semantics, estimate_bytes):
    return pltpu.CompilerParams(dimension_semantics=semantics,
                                vmem_limit_bytes=_vmem_limit(estimate_bytes))


def _rms(x, g):
    ms = jnp.mean(x * x, axis=-1, keepdims=True)
    return x * lax.rsqrt(ms + EPS) * g


def _inproj_kernel(*refs, tn, n_riders):
    x_ref, g_ref, w_ref = refs[:3]
    rider_src = refs[3:3 + n_riders]
    o_ref, slab_ref = refs[3 + n_riders:5 + n_riders]
    rider_dst = refs[5 + n_riders:5 + 2 * n_riders]
    hn_ref = refs[5 + 2 * n_riders]
    for src, dst in zip(rider_src, rider_dst):
        dst[...] = src[...].astype(dst.dtype)
    hn_ref[...] = _rms(x_ref[...], g_ref[...]).astype(BF16)
    n_main = o_ref.shape[1]
    w = REC_BLOCK_W
    for c in range(0, w_ref.shape[1], tn):
        res = jnp.dot(hn_ref[...], w_ref[:, c:c + tn], preferred_element_type=F32).astype(o_ref.dtype)
        if c < n_main:
            o_ref[:, c:c + tn] = res
        else:
            for k in range(tn // w):
                slab_ref[(c - n_main) // w + k] = res[:, k * w:(k + 1) * w]


def _rider_block(rows, steps):
    share = 1
    while (rows * share) % (steps * V7X_BF16_ROWS) != 0:
        share *= 2
    return rows * share // steps, share


def _in_proj(x2, g, w, riders, *, n_main, tm, tn):
    t, d = x2.shape
    n = w.shape[1]
    assert n_main % tn == 0 and (n - n_main) % tn == 0 and tn % REC_BLOCK_W == 0
    n_slabs = (n - n_main) // REC_BLOCK_W
    steps = t // tm
    est = 2 * tm * d * 4 + d * n * 2 + 2 * tm * n * 2 + tm * d * 2 + tm * d * 4
    rider_specs = []
    for r in riders:
        rows, share = _rider_block(r.shape[0], steps)
        rider_specs.append(pl.BlockSpec((rows, r.shape[1]), lambda i, share=share: (i // share, 0)))
        est += 2 * rows * r.shape[1] * (4 + 2)
    out = pl.pallas_call(
        functools.partial(_inproj_kernel, tn=tn, n_riders=len(riders)),
        out_shape=[jax.ShapeDtypeStruct((t, n_main), BF16),
                   jax.ShapeDtypeStruct((n_slabs, t, REC_BLOCK_W), BF16)]
                  + [jax.ShapeDtypeStruct(r.shape, BF16) for r in riders],
        grid=(steps,),
        in_specs=[pl.BlockSpec((tm, d), lambda i: (i, 0)),
                  pl.BlockSpec((1, d), lambda i: (0, 0)),
                  pl.BlockSpec((d, n), lambda i: (0, 0), pipeline_mode=pl.Buffered(1))] + rider_specs,
        out_specs=[pl.BlockSpec((tm, n_main), lambda i: (i, 0)),
                   pl.BlockSpec((n_slabs, tm, REC_BLOCK_W), lambda i: (0, i, 0))] + rider_specs,
        scratch_shapes=[pltpu.VMEM((tm, d), BF16)],
        compiler_params=_params(("arbitrary",), est),
        name="in_proj",
    )(x2, g, w, *riders)
    return out[0], out[1], out[2:]


def _attn_window(rows):
    nblk = rows // Q_ROWS
    kr = min(WIN_R, rows)
    n_dr = 2 * WIN_R - 1
    q_row0 = np.array([0, Q_ROWS, rows - Q_ROWS])
    k_row0 = np.array([0, 0, (nblk - K_BLOCKS) * Q_ROWS])
    qr = q_row0[:, None] + np.arange(Q_ROWS)[None, :]
    kro = k_row0[:, None] + np.arange(K_BLOCKS * Q_ROWS)[None, :]
    rstart = np.clip(qr - kr // 2, 0, rows - kr)
    row_ok = (kro[:, None, :] >= rstart[:, :, None]) & (kro[:, None, :] < rstart[:, :, None] + kr)
    dr = np.clip(kro[:, None, :] - qr[:, :, None] + WIN_R - 1, 0, n_dr - 1)
    return row_ok, dr


def _bias_kernel(rpb_ref, o_ref, *, row_ok, dr):
    lanes = 2 * GRID_W
    qc = lax.broadcasted_iota(jnp.int32, (GRID_W, lanes), 0)
    lane = lax.broadcasted_iota(jnp.int32, (GRID_W, lanes), 1)
    kc = lane & (GRID_W - 1)
    cstart = jnp.clip(qc - WIN_C // 2, 0, GRID_W - WIN_C)
    col_ok = (kc >= cstart) & (kc < cstart + WIN_C)
    first = lane < GRID_W
    neg = jnp.full((GRID_W, lanes), NEG_INF, F32)

    toeplitz = {}

    def block(r, second):
        if (r, second) not in toeplitz:
            row = jnp.broadcast_to(rpb_ref[r:r + 1, :], (GRID_W, lanes))
            t = pltpu.roll(row, lanes - (WIN_C - 1), 1, stride=1, stride_axis=0)
            toeplitz[(r, False)] = t
            toeplitz[(r, True)] = pltpu.roll(t, GRID_W, 1)
        return toeplitz[(r, second)]

    for t in range(3):
        for a in range(Q_ROWS):
            for pair in range(K_BLOCKS * Q_ROWS // 2):
                l0, l1 = 2 * pair, 2 * pair + 1
                left = block(int(dr[t, a, l0]), False) if row_ok[t, a, l0] else neg
                right = block(int(dr[t, a, l1]), True) if row_ok[t, a, l1] else neg
                tile = jnp.where(col_ok, jnp.where(first, left, right), neg)
                o_ref[t, a * GRID_W:(a + 1) * GRID_W, pair * lanes:(pair + 1) * lanes] = tile


def _attn_bias(rpb, rows):
    n_heads, n_dr, n_dc = rpb.shape
    row_ok, dr = _attn_window(rows)
    padded = jnp.zeros((n_heads, -(-n_dr // V7X_SUBLANES) * V7X_SUBLANES, 2 * GRID_W), F32)
    padded = padded.at[:, :n_dr, :n_dc].set(rpb.astype(F32))
    shape = (3, n_heads, Q_TOK, K_BLOCKS * Q_TOK)
    return pl.pallas_call(
        functools.partial(_bias_kernel, row_ok=row_ok, dr=dr),
        out_shape=jax.ShapeDtypeStruct(shape, F32),
        grid=(n_heads,),
        in_specs=[pl.BlockSpec((None,) + padded.shape[1:], lambda h: (h, 0, 0))],
        out_specs=pl.BlockSpec((3, None) + shape[2:], lambda h: (0, h, 0, 0)),
        compiler_params=_params(("parallel",), 2 * 3 * shape[2] * shape[3] * 4),
        name="attn_bias",
    )(padded)


def _attn_heads(q_ref, k_refs, v_refs, bias_ref, g_ref, o_ref, *, n_heads, need):
    scale = HEAD_DIM ** -0.5
    n_tiles = len(need[0])
    per_blk = n_tiles // K_BLOCKS
    zeros = jnp.zeros((GRID_W, HEAD_DIM), BF16)
    for h in range(n_heads):
        hs = slice(h * HEAD_DIM, (h + 1) * HEAD_DIM)
        q = (q_ref[:, hs].astype(F32) * scale).astype(BF16)
        s = [lax.dot_general(q, k_refs[i][:, hs], (((1,), (1,)), ((), ())),
                             preferred_element_type=F32) for i in range(K_BLOCKS)]
        p_rows, l_rows = [], []
        for a in range(Q_ROWS):
            rs = slice(a * GRID_W, (a + 1) * GRID_W)
            tiles = {t: s[t // per_blk][rs, (t % per_blk) * HEAD_DIM:(t % per_blk + 1) * HEAD_DIM]
                        + bias_ref[h, rs, t * HEAD_DIM:(t + 1) * HEAD_DIM]
                     for t in range(n_tiles) if need[a][t]}
            m = functools.reduce(jnp.maximum, tiles.values()).max(axis=-1, keepdims=True)
            p = {t: jnp.exp(v - m) for t, v in tiles.items()}
            l_rows.append(functools.reduce(jnp.add, p.values()).sum(axis=-1, keepdims=True))
            p_rows.append([p[t].astype(BF16) if t in p else zeros for t in range(n_tiles)])
        acc = None
        for i in range(K_BLOCKS):
            p_blk = jnp.concatenate(
                [jnp.concatenate(row[i * per_blk:(i + 1) * per_blk], axis=1) for row in p_rows], axis=0)
            part = jnp.dot(p_blk, v_refs[i][:, hs], preferred_element_type=F32)
            acc = part if acc is None else acc + part
        o_ref[:, hs] = (acc * (1.0 / jnp.concatenate(l_rows, axis=0))).astype(o_ref.dtype)
    o_ref[...] = _rms(o_ref[...].astype(F32), g_ref[...]).astype(o_ref.dtype)


def _attn_kernel(q_ref, k0_ref, k1_ref, k2_ref, v0_ref, v1_ref, v2_ref, bias_ref, g_ref, o_ref, *,
                 n_heads, need):
    j = pl.program_id(0)
    last = pl.num_programs(0) - 1
    block_type = jnp.where(j == 0, 0, jnp.where(j == last, 2, 1))
    for t in range(3):
        @pl.when(block_type == t)
        def _():
            _attn_heads(q_ref, (k0_ref, k1_ref, k2_ref), (v0_ref, v1_ref, v2_ref), bias_ref, g_ref, o_ref,
                        n_heads=n_heads, need=need[t])


def _attention(u, bias, g_out, *, batch, seq, d_attn):
    t = u.shape[0]
    row_ok, _ = _attn_window(seq // GRID_W)
    pairs = row_ok.reshape(3, Q_ROWS, -1, HEAD_DIM // GRID_W)
    need = tuple(tuple(tuple(bool(x) for x in row) for row in typ) for typ in pairs.any(axis=-1))
    n_heads = d_attn // HEAD_DIM
    nblk = seq // Q_TOK
    assert nblk >= K_BLOCKS

    def q_map(j, b):
        return (b * nblk + j, 0)

    def kv_map(i, col):
        def f(j, b):
            return (b * nblk + jnp.clip(j - 1, 0, nblk - K_BLOCKS) + i, col)
        return f

    def bias_map(j, b):
        return (jnp.where(j == 0, 0, jnp.where(j == nblk - 1, 2, 1)), 0, 0, 0)

    blk = pl.BlockSpec((Q_TOK, d_attn), q_map)
    in_specs = [blk]
    in_specs += [pl.BlockSpec((Q_TOK, d_attn), kv_map(i, 1)) for i in range(K_BLOCKS)]
    in_specs += [pl.BlockSpec((Q_TOK, d_attn), kv_map(i, 2)) for i in range(K_BLOCKS)]
    in_specs += [pl.BlockSpec((None, n_heads, Q_TOK, K_BLOCKS * Q_TOK), bias_map)]
    in_specs += [pl.BlockSpec((1, d_attn), lambda j, b: (0, 0))]
    est = 2 * 8 * Q_TOK * d_attn * 2 + 2 * n_heads * Q_TOK * K_BLOCKS * Q_TOK * 4
    return pl.pallas_call(
        functools.partial(_attn_kernel, n_heads=n_heads, need=need),
        out_shape=jax.ShapeDtypeStruct((t, d_attn), BF16),
        grid=(nblk, batch),
        in_specs=in_specs,
        out_specs=blk,
        compiler_params=_params(("arbitrary", "arbitrary"), est),
        name="nattn",
    )(u, u, u, u, u, u, u, bias, g_out)


def _scan_pitch(n_chunks):
    return n_chunks + 4 if (n_chunks // 4) % 2 == 0 else n_chunks + 8


def _sigmoid(x):
    return 0.5 * jnp.tanh(0.5 * x) + 0.5


def _rglru_kernel(xr_ref, yg_ref, cw_ref, cb_ref, wg_ref, bg_ref, lam_ref, o_ref,
                  xp_ref, a_ref, b_ref, h_ref, cr_ref, *, seq):
    n_chunks = seq // SCAN_CHUNK
    pitch = _scan_pitch(n_chunks)
    w = REC_BLOCK_W
    halo = V7X_BF16_ROWS

    zeros = jnp.zeros((halo, w), xp_ref.dtype)
    xp_ref[0:halo, :] = zeros
    xp_ref[seq + halo:seq + 2 * halo, :] = zeros

    def stage(c, carry):
        r = pl.multiple_of(c * SCAN_CHUNK, SCAN_CHUNK)
        xp_ref[pl.ds(r + halo, SCAN_CHUNK), :] = xr_ref[pl.ds(r, SCAN_CHUNK), :].astype(F32)
        return carry
    lax.fori_loop(0, n_chunks, stage, 0)

    z = -lam_ref[...]
    half_coef = (-0.5 * C_RG * LOG2_E) * (jnp.maximum(z, 0.0) + jnp.log1p(jnp.exp(-jnp.abs(z))))

    left = CONV_W // 2

    def gates(c, carry):
        r = pl.multiple_of(c * SCAN_CHUNK, SCAN_CHUNK)
        xc = cb_ref[...]
        for j in range(CONV_W):
            xc = xc + xp_ref[pl.ds(r + (halo + j - left), SCAN_CHUNK), :] * cw_ref[j:j + 1, :]
        th = jnp.tanh(jnp.dot(xc.astype(BF16), wg_ref[...], preferred_element_type=F32) + bg_ref[...])
        xh = 0.5 * xc
        chunk_rows = pl.ds(c, SCAN_CHUNK, stride=pitch)
        for d in range(2):
            a = jnp.exp2(th[:, d * w:(d + 1) * w] * half_coef[d:d + 1, :] + half_coef[d:d + 1, :])
            y = 1.0 - a * a
            root = jnp.where(y > 0.0, y * lax.rsqrt(y), 0.0)
            a_ref[d, chunk_rows, :] = a
            b_ref[d, chunk_rows, :] = root * (th[:, (2 + d) * w:(3 + d) * w] + 1.0) * xh
        return carry
    lax.fori_loop(0, n_chunks, gates, 0, unroll=GATE_UNROLL)

    def step_rows(d, t):
        pos = t if d == 0 else SCAN_CHUNK - 1 - t
        return pl.ds(pos * pitch, n_chunks)

    def pass1(t, carry):
        hs, ps = carry
        a = [a_ref[d, step_rows(d, t), :] for d in range(2)]
        return (tuple(a[d] * hs[d] + b_ref[d, step_rows(d, t), :] for d in range(2)),
                tuple(a[d] * ps[d] for d in range(2)))

    zero = jnp.zeros((n_chunks, w), F32)
    he, pe = lax.fori_loop(0, SCAN_CHUNK, pass1, ((zero, zero), (zero + 1.0, zero + 1.0)),
                           unroll=SCAN_UNROLL)

    for d in range(2):
        order = range(n_chunks) if d == 0 else range(n_chunks - 1, -1, -1)
        carry = jnp.zeros((1, w), F32)
        for j in order:
            cr_ref[d, j:j + 1, :] = carry
            carry = he[d][j:j + 1, :] + pe[d][j:j + 1, :] * carry

    def pass2(t, hs):
        nh = []
        for d in range(2):
            h = a_ref[d, step_rows(d, t), :] * hs[d] + b_ref[d, step_rows(d, t), :]
            h_ref[d, step_rows(d, t), :] = h
            nh.append(h)
        return tuple(nh)

    lax.fori_loop(0, SCAN_CHUNK, pass2, (cr_ref[0], cr_ref[1]), unroll=SCAN_UNROLL)

    def combine(c, carry):
        r = pl.multiple_of(c * SCAN_CHUNK, SCAN_CHUNK)
        chunk_rows = pl.ds(c, SCAN_CHUNK, stride=pitch)
        h = h_ref[0, chunk_rows, :] + h_ref[1, chunk_rows, :]
        y = yg_ref[pl.ds(r, SCAN_CHUNK), :].astype(F32)
        o_ref[pl.ds(r, SCAN_CHUNK), :] = (h * jax.nn.gelu(y)).astype(o_ref.dtype)
        return carry
    lax.fori_loop(0, n_chunks, combine, 0)


def _rglru(slabs, conv_w, conv_b, wg, bg, lam, *, d_rec):
    _, b, s, w = slabs.shape
    nb = d_rec // w
    assert s % (SCAN_CHUNK * V7X_SUBLANES) == 0
    n_chunks = s // SCAN_CHUNK
    scan_buf = pltpu.VMEM((2, SCAN_CHUNK * _scan_pitch(n_chunks), w), F32)
    est = (3 * 2 * s * w * 2 + (s + 2 * V7X_BF16_ROWS) * w * 4
           + 3 * 2 * SCAN_CHUNK * _scan_pitch(n_chunks) * w * 4)
    return pl.pallas_call(
        functools.partial(_rglru_kernel, seq=s),
        out_shape=jax.ShapeDtypeStruct((b, s, d_rec), BF16),
        grid=(b, nb),
        in_specs=[pl.BlockSpec((None, None, s, w), lambda i, n: (n, i, 0, 0)),
                  pl.BlockSpec((None, None, s, w), lambda i, n: (nb + n, i, 0, 0)),
                  pl.BlockSpec((CONV_W, w), lambda i, n: (0, n)),
                  pl.BlockSpec((1, w), lambda i, n: (0, n)),
                  pl.BlockSpec((None, w, 4 * w), lambda i, n: (n, 0, 0)),
                  pl.BlockSpec((None, 1, 4 * w), lambda i, n: (n, 0, 0)),
                  pl.BlockSpec((2, w), lambda i, n: (0, n))],
        out_specs=pl.BlockSpec((None, s, w), lambda i, n: (i, 0, n)),
        scratch_shapes=[pltpu.VMEM((s + 2 * V7X_BF16_ROWS, w), F32),
                        scan_buf, scan_buf, scan_buf, pltpu.VMEM((2, n_chunks, w), F32)],
        compiler_params=_params(("parallel", "arbitrary"), est),
        name="rglru",
    )(slabs, slabs, conv_w, conv_b, wg, bg, lam)


def _outproj_kernel(a_ref, r_ref, x_ref, gr_ref, w_ref, gp_ref, o_ref, cat_ref):
    da = a_ref.shape[1]
    cat_ref[:, :da] = a_ref[...]
    cat_ref[:, da:] = _rms(r_ref[...].astype(F32), gr_ref[...]).astype(BF16)
    mixed = jnp.dot(cat_ref[...], w_ref[...], preferred_element_type=F32)
    o_ref[...] = x_ref[...] + _rms(mixed, gp_ref[...])


def _out_proj(attn, rec, x2, g_rec, w, g_post, *, tm):
    t, d = x2.shape
    da, dr = attn.shape[1], rec.shape[1]
    est = 2 * tm * (da + dr) * 2 + 4 * tm * d * 4 + (da + dr) * d * 2 + tm * (da + dr) * 2
    row = lambda i: (i, 0)
    fixed = lambda i: (0, 0)
    return pl.pallas_call(
        _outproj_kernel,
        out_shape=jax.ShapeDtypeStruct((t, d), F32),
        grid=(t // tm,),
        in_specs=[pl.BlockSpec((tm, da), row), pl.BlockSpec((tm, dr), row), pl.BlockSpec((tm, d), row),
                  pl.BlockSpec((1, dr), fixed),
                  pl.BlockSpec((da + dr, d), fixed, pipeline_mode=pl.Buffered(1)),
                  pl.BlockSpec((1, d), fixed)],
        out_specs=pl.BlockSpec((tm, d), row),
        scratch_shapes=[pltpu.VMEM((tm, da + dr), BF16)],
        compiler_params=_params(("parallel",), est),
        name="out_proj",
    )(attn, rec, x2, g_rec, w, g_post)


def _ffn_kernel(h_ref, gpre_ref, wg_ref, wu_ref, wd_ref, gpost_ref, o_ref, fn_ref, acc_ref, *, row_chunk):
    f = pl.program_id(1)

    @pl.when((f == 0) & (pl.program_id(0) == 0))
    def _():
        acc_ref[...] = jnp.zeros(acc_ref.shape, acc_ref.dtype)

    @pl.when(f == 0)
    def _():
        def body(c, carry):
            r = pl.multiple_of(c * row_chunk, row_chunk)
            fn_ref[pl.ds(r, row_chunk), :] = _rms(h_ref[pl.ds(r, row_chunk), :], gpre_ref[...]).astype(BF16)
            return carry
        lax.fori_loop(0, h_ref.shape[0] // row_chunk, body, 0, unroll=2)

    fn = fn_ref[...]
    acts = []
    for c in range(0, wg_ref.shape[1], V7X_MXU_COLS):
        gate = jnp.dot(fn, wg_ref[:, c:c + V7X_MXU_COLS], preferred_element_type=F32)
        up = jnp.dot(fn, wu_ref[:, c:c + V7X_MXU_COLS], preferred_element_type=F32)
        acts.append((jax.nn.silu(gate) * up).astype(BF16))
    for k in range(len(acts)):
        part = jnp.dot(acts[k], wd_ref[k * V7X_MXU_COLS:(k + 1) * V7X_MXU_COLS, :], preferred_element_type=F32)
        prev = acc_ref[...]
        if k == 0:
            prev = jnp.where(f > 0, prev, 0.0)
        acc_ref[...] = prev + part

    @pl.when(f == pl.num_programs(1) - 1)
    def _():
        def body(c, carry):
            r = pl.multiple_of(c * row_chunk, row_chunk)
            o_ref[pl.ds(r, row_chunk), :] = _rms(acc_ref[pl.ds(r, row_chunk), :], gpost_ref[...]).astype(o_ref.dtype)
            return carry
        lax.fori_loop(0, h_ref.shape[0] // row_chunk, body, 0, unroll=2)


def _ffn(h1, g_pre, w_gate, w_up, w_down, g_post, *, tm, tf):
    t, d = h1.shape
    dff = w_gate.shape[1]
    est = 2 * tm * d * 4 + 3 * 2 * d * tf * 2 + 2 * tm * d * 2 + tm * d * 2 + tm * d * 4
    return pl.pallas_call(
        functools.partial(_ffn_kernel, row_chunk=min(128, tm)),
        out_shape=jax.ShapeDtypeStruct((t, d), BF16),
        grid=(t // tm, dff // tf),
        in_specs=[pl.BlockSpec((tm, d), lambda i, f: (i, 0)),
                  pl.BlockSpec((1, d), lambda i, f: (0, 0)),
                  pl.BlockSpec((d, tf), lambda i, f: (0, f)),
                  pl.BlockSpec((d, tf), lambda i, f: (0, f)),
                  pl.BlockSpec((tf, d), lambda i, f: (f, 0)),
                  pl.BlockSpec((1, d), lambda i, f: (0, 0))],
        out_specs=pl.BlockSpec((tm, d), lambda i, f: (i, 0)),
        scratch_shapes=[pltpu.VMEM((tm, d), BF16), pltpu.VMEM((tm, d), F32)],
        compiler_params=_params(("parallel", "arbitrary"), est),
        name="ffn",
    )(h1, g_pre, w_gate, w_up, w_down, g_post)


def _ple_kernel(h_ref, d_ref, p_ref, gpre_ref, wg_ref, wp_ref, gpost_ref, o_ref):
    h2 = h_ref[...] + d_ref[...].astype(F32)
    gate = _sigmoid(jnp.dot(_rms(h2, gpre_ref[...]).astype(BF16), wg_ref[...],
                            preferred_element_type=F32))
    ple = jnp.dot(p_ref[...].astype(BF16), wp_ref[...], preferred_element_type=F32)
    o_ref[...] = h2 + _rms(gate * ple, gpost_ref[...])


def _ple(h1, dff, p2, g_pre, w_gate, w_proj, g_post, *, tm):
    t, d = h1.shape
    dp = p2.shape[1]
    est = 4 * tm * d * 4 + 2 * tm * d * 2 + 2 * tm * dp * 4 + d * d * 2 + 2 * dp * d * 2 + 3 * tm * d * 4
    row = lambda i: (i, 0)
    fixed = lambda i: (0, 0)
    return pl.pallas_call(
        _ple_kernel,
        out_shape=jax.ShapeDtypeStruct((t, d), F32),
        grid=(t // tm,),
        in_specs=[pl.BlockSpec((tm, d), row), pl.BlockSpec((tm, d), row), pl.BlockSpec((tm, dp), row),
                  pl.BlockSpec((1, d), fixed),
                  pl.BlockSpec((d, d), fixed, pipeline_mode=pl.Buffered(1)),
                  pl.BlockSpec((dp, d), fixed),
                  pl.BlockSpec((1, d), fixed)],
        out_specs=pl.BlockSpec((tm, d), row),
        compiler_params=_params(("parallel",), est),
        name="ple",
    )(h1, dff, p2, g_pre, w_gate, w_proj, g_post)


def _layer(h, p_i, g_mix_pre, w_in, rpb, conv_w, conv_b, w_rg_a, b_rg_a, w_rg_i, b_rg_i, lam,
           g_attn_out, g_rec_out, w_out, g_mix_post, g_ffn_pre, w_ffn_gate, w_ffn_up, w_ffn_down,
           g_ffn_post, g_ple_pre, w_ple_gate, w_ple_proj, g_ple_post):
    batch, seq, d = h.shape
    t = batch * seq
    d_attn = g_attn_out.shape[0]
    d_rec = g_rec_out.shape[0]
    nb = d_rec // REC_BLOCK_W
    row = lambda v: v.reshape(1, -1).astype(F32)

    x2 = h.reshape(t, d)
    u, u_rec, (w_out_b, w_gate_b, w_up_b, w_down_b, w_pgate_b) = _in_proj(
        x2, row(g_mix_pre), w_in.astype(BF16), [w_out, w_ffn_gate, w_ffn_up, w_ffn_down, w_ple_gate],
        n_main=3 * d_attn, tm=min(512, t), tn=1024)

    attn = _attention(u, _attn_bias(rpb, seq // GRID_W), row(g_attn_out), batch=batch, seq=seq, d_attn=d_attn)

    wg = (0.5 * jnp.concatenate([w_rg_a[0], w_rg_a[1], w_rg_i[0], w_rg_i[1]], axis=-1)).astype(BF16)
    bg = jnp.concatenate([b_rg_a.reshape(2, nb, 1, REC_BLOCK_W)[0], b_rg_a.reshape(2, nb, 1, REC_BLOCK_W)[1],
                          b_rg_i.reshape(2, nb, 1, REC_BLOCK_W)[0], b_rg_i.reshape(2, nb, 1, REC_BLOCK_W)[1]],
                         axis=-1).astype(F32) * 0.5
    rec = _rglru(u_rec.reshape(-1, batch, seq, REC_BLOCK_W), conv_w.astype(F32), row(conv_b), wg, bg,
                 lam.astype(F32), d_rec=d_rec)

    h1 = _out_proj(attn, rec.reshape(t, d_rec), x2, row(g_rec_out),
                   w_out_b, row(g_mix_post), tm=min(512, t))

    dff = _ffn(h1, row(g_ffn_pre), w_gate_b, w_up_b,
               w_down_b, row(g_ffn_post), tm=min(1024, t), tf=512)

    out = _ple(h1, dff, p_i.reshape(t, -1), row(g_ple_pre), w_pgate_b,
               w_ple_proj.astype(BF16), row(g_ple_post), tm=min(512, t))
    return out.reshape(batch, seq, d)


def kernel(x, p, g_mix_pre, w_in, rpb, conv_w, conv_b, w_rg_a, b_rg_a, w_rg_i, b_rg_i, lam, g_attn_out, g_rec_out, w_out, g_mix_post, g_ffn_pre, w_ffn_gate, w_ffn_up, w_ffn_down, g_ffn_post, g_ple_pre, w_ple_gate, w_ple_proj, g_ple_post):
    h = x
    for i in range(p.shape[0]):
        h = _layer(h, p[i], g_mix_pre[i], w_in[i], rpb[i], conv_w[i], conv_b[i], w_rg_a[i], b_rg_a[i],
                   w_rg_i[i], b_rg_i[i], lam[i], g_attn_out[i], g_rec_out[i], w_out[i], g_mix_post[i],
                   g_ffn_pre[i], w_ffn_gate[i], w_ffn_up[i], w_ffn_down[i], g_ffn_post[i], g_ple_pre[i],
                   w_ple_gate[i], w_ple_proj[i], g_ple_post[i])
    return h
```
